```python
import jax, jax.numpy as jnp
from jax import lax
import numpy as np

D_MODEL = 2048
BATCH = 16
SEQ = 2048
DEPTH = 1
DEC_BATCH = 32
DEC_SEQ = 8
PAST_LEN = 16384
PAGE_SIZE = 128

HEAD_DIM = 128
POOL_WIDTH = D_MODEL // 4
POOL_WINDOWS = (2, 4, 8, 16)
POOL_GROUP = POOL_WIDTH // len(POOL_WINDOWS)
POOL_CTX = max(POOL_WINDOWS) - 1
MEM_HEADS = 4
MEM_WIDTH = MEM_HEADS * HEAD_DIM
N_HEADS = (D_MODEL - POOL_WIDTH - MEM_WIDTH) // HEAD_DIM
ATTN_WIDTH = N_HEADS * HEAD_DIM
MIX_WIDTH = POOL_WIDTH + ATTN_WIDTH + MEM_WIDTH
IN_WIDTH = POOL_WIDTH + 3 * ATTN_WIDTH + MEM_WIDTH
N_MEM = 256
MOBA_BLOCK = 256
MOBA_TOPK = 3
QUERY_CHUNK = 16
FFN_DIM = ((8 * D_MODEL // 3 + 255) // 256) * 256
CONV_WIDTH = 3
CONV_CTX = CONV_WIDTH - 1
NORM_EPS = 1e-6
NEG_INF = -1e30

kernel_name = "hymba_pool_moba_mem_convffn_step"


def rms_norm(x, g):
    xf = x.astype(jnp.float32)
    y = xf * lax.rsqrt(jnp.mean(xf * xf, axis=-1, keepdims=True) + NORM_EPS)
    return (y * g.astype(jnp.float32)).astype(x.dtype)


def alibi_slopes():
    return jnp.exp2(-8.0 * jnp.arange(1, N_HEADS + 1, dtype=jnp.float32) / N_HEADS)


def project(x, norm_mix, w_in, q_gain, k_gain, mq_gain):
    B, T, _ = x.shape
    z = rms_norm(x, norm_mix) @ w_in
    o1 = POOL_WIDTH
    o2 = o1 + ATTN_WIDTH
    o3 = o2 + ATTN_WIDTH
    o4 = o3 + ATTN_WIDTH
    u = z[..., :o1]
    q = rms_norm(z[..., o1:o2].reshape(B, T, N_HEADS, HEAD_DIM), q_gain)
    k = rms_norm(z[..., o2:o3].reshape(B, T, N_HEADS, HEAD_DIM), k_gain)
    v = z[..., o3:o4].reshape(B, T, N_HEADS, HEAD_DIM)
    mq = rms_norm(z[..., o4:].reshape(B, T, MEM_HEADS, HEAD_DIM), mq_gain)
    return u, q, k, v, mq


def pool_mix(u, prev, pos0, pool_w, pool_scale):
    B, T, _ = u.shape
    ext = jnp.concatenate([prev.astype(u.dtype), u], axis=1)
    cs = jnp.cumsum(ext.astype(jnp.float32), axis=1)
    cs = jnp.concatenate([jnp.zeros((B, 1, POOL_WIDTH), jnp.float32), cs], axis=1)
    pos = pos0 + jnp.arange(T)
    means = []
    for g, w in enumerate(POOL_WINDOWS):
        ch = slice(g * POOL_GROUP, (g + 1) * POOL_GROUP)
        win = cs[:, POOL_CTX + 1:POOL_CTX + 1 + T, ch] - cs[:, POOL_CTX + 1 - w:POOL_CTX + 1 - w + T, ch]
        cnt = jnp.minimum(pos + 1, w).astype(jnp.float32)[None, :, None]
        means.append(win / cnt)
    mean = jnp.stack(means, axis=2)
    diff = (mean - u.reshape(B, T, len(POOL_WINDOWS), POOL_GROUP).astype(jnp.float32)).astype(u.dtype)
    y = jnp.einsum('btgc,gcd->btgd', diff, pool_w).reshape(B, T, POOL_WIDTH) * pool_scale
    return y, ext[:, -POOL_CTX:]


def moba_attend(q, q_pos, k_own, v_own, own_pos, slopes, k_sel=None, v_sel=None, sel_pos=None, sel_valid=None):
    scale = HEAD_DIM ** -0.5
    s_own = jnp.einsum('bhtd,bkhd->bhtk', q, k_own).astype(jnp.float32) * scale
    s_own = s_own - slopes[:, None, None] * (q_pos[:, None] - own_pos[None, :]).astype(jnp.float32)
    s_own = jnp.where(own_pos[None, :] <= q_pos[:, None], s_own, NEG_INF)
    if k_sel is None:
        p = jax.nn.softmax(s_own, axis=-1).astype(v_own.dtype)
        return jnp.einsum('bhtk,bkhd->bhtd', p, v_own)
    B, H, Tq, R, L, _ = k_sel.shape
    s_sel = jnp.einsum('bhtd,bhtrld->bhtrl', q, k_sel).astype(jnp.float32) * scale
    s_sel = s_sel - slopes[:, None, None, None] * (q_pos[:, None, None] - sel_pos).astype(jnp.float32)
    s_sel = jnp.where(sel_valid[..., None], s_sel, NEG_INF).reshape(B, H, Tq, R * L)
    p = jax.nn.softmax(jnp.concatenate([s_sel, s_own], axis=-1), axis=-1).astype(v_own.dtype)
    out = jnp.einsum('bhtk,bhtkd->bhtd', p[..., :R * L], v_sel.reshape(B, H, Tq, R * L, HEAD_DIM))
    return out + jnp.einsum('bhtk,bkhd->bhtd', p[..., R * L:], v_own)


def moba_prompt(q, k, v, slopes):
    B, T, H, _ = q.shape
    nblk = -(-T // MOBA_BLOCK)
    pad = nblk * MOBA_BLOCK - T
    kb = jnp.pad(k, ((0, 0), (0, pad), (0, 0), (0, 0))).reshape(B, nblk, MOBA_BLOCK, H, HEAD_DIM)
    vb = jnp.pad(v, ((0, 0), (0, pad), (0, 0), (0, 0))).reshape(B, nblk, MOBA_BLOCK, H, HEAD_DIM)
    R = min(MOBA_TOPK, nblk - 1)
    qh = q.transpose(0, 2, 1, 3)
    pos = jnp.arange(T)
    if R > 0:
        kmean = jnp.mean(kb.astype(jnp.float32), axis=2)
        gate = jnp.einsum('bhtd,bnhd->bhtn', qh.astype(jnp.float32), kmean)
        n_full = pos // MOBA_BLOCK
        gate = jnp.where(jnp.arange(nblk)[None, :] < n_full[:, None], gate, NEG_INF)
        _, idx = lax.top_k(gate, R)
        valid = jnp.arange(R)[None, :] < n_full[:, None]
        kbh = kb.transpose(0, 3, 1, 2, 4)
        vbh = vb.transpose(0, 3, 1, 2, 4)
        bi = jnp.arange(B)[:, None, None, None]
        hi = jnp.arange(H)[None, :, None, None]

    def chunk(c):
        t0 = c * QUERY_CHUNK
        qc = lax.dynamic_slice_in_dim(qh, t0, QUERY_CHUNK, axis=2)
        qpos = t0 + jnp.arange(QUERY_CHUNK)
        blk = t0 // MOBA_BLOCK
        k_own = lax.dynamic_index_in_dim(kb, blk, axis=1, keepdims=False)
        v_own = lax.dynamic_index_in_dim(vb, blk, axis=1, keepdims=False)
        own_pos = blk * MOBA_BLOCK + jnp.arange(MOBA_BLOCK)
        if R == 0:
            return moba_attend(qc, qpos, k_own, v_own, own_pos, slopes)
        ic = lax.dynamic_slice_in_dim(idx, t0, QUERY_CHUNK, axis=2)
        k_sel = kbh[bi, hi, ic]
        v_sel = vbh[bi, hi, ic]
        sel_pos = ic[..., None] * MOBA_BLOCK + jnp.arange(MOBA_BLOCK)
        vc = lax.dynamic_slice_in_dim(valid, t0, QUERY_CHUNK, axis=0)
        return moba_attend(qc, qpos, k_own, v_own, own_pos, slopes, k_sel, v_sel, sel_pos, vc)

    out = lax.map(chunk, jnp.arange(T // QUERY_CHUNK))
    return out.transpose(1, 0, 3, 2, 4).reshape(B, T, H * HEAD_DIM)


def moba_sample(q, k_new, v_new, cache_k, cache_v, layer, page_table, slopes):
    DB, Tn, H, _ = q.shape
    n_pages = page_table.shape[1]
    past = n_pages * PAGE_SIZE
    n_full = past // MOBA_BLOCK
    own_start = n_full * MOBA_BLOCK
    n_tail = past - own_start
    pages_per_block = MOBA_BLOCK // PAGE_SIZE
    R = min(MOBA_TOPK, n_full)
    qh = q.transpose(0, 2, 1, 3)
    q_pos = past + jnp.arange(Tn)
    k_past = cache_k[layer, page_table].reshape(DB, past, H, HEAD_DIM)
    v_tail = cache_v[layer, page_table[:, own_start // PAGE_SIZE:]].reshape(DB, n_tail, H, HEAD_DIM)
    k_own = jnp.concatenate([k_past[:, own_start:].astype(k_new.dtype), k_new], axis=1)
    v_own = jnp.concatenate([v_tail.astype(v_new.dtype), v_new], axis=1)
    own_pos = own_start + jnp.arange(n_tail + Tn)
    if R == 0:
        out = moba_attend(qh, q_pos, k_own, v_own, own_pos, slopes)
    else:
        kb = k_past[:, :own_start].reshape(DB, n_full, MOBA_BLOCK, H, HEAD_DIM)
        kmean = jnp.mean(kb.astype(jnp.float32), axis=2)
        gate = jnp.einsum('bhtd,bnhd->bhtn', qh.astype(jnp.float32), kmean)
        _, idx = lax.top_k(gate, R)
        bi = jnp.arange(DB)[:, None, None, None]
        hi = jnp.arange(H)[None, :, None, None]
        k_sel = kb[bi, idx, :, hi, :].astype(k_new.dtype)
        phys = page_table[bi[..., None], idx[..., None] * pages_per_block + jnp.arange(pages_per_block)]
        lyr = jnp.full_like(phys, layer)
        v_sel = cache_v[lyr, phys, :, hi[..., None], :].reshape(DB, H, Tn, R, MOBA_BLOCK, HEAD_DIM).astype(v_new.dtype)
        sel_pos = idx[..., None] * MOBA_BLOCK + jnp.arange(MOBA_BLOCK)
        sel_valid = jnp.ones((Tn, R), dtype=bool)
        out = moba_attend(qh, q_pos, k_own, v_own, own_pos, slopes, k_sel, v_sel, sel_pos, sel_valid)
    return out.transpose(0, 2, 1, 3).reshape(DB, Tn, H * HEAD_DIM)


def mem_kv(mem, norm_mem, w_mem_kv, mk_gain):
    B, M, _ = mem.shape
    kv = rms_norm(mem, norm_mem) @ w_mem_kv
    k = rms_norm(kv[..., :MEM_WIDTH].reshape(B, M, MEM_HEADS, HEAD_DIM), mk_gain)
    v = kv[..., MEM_WIDTH:].reshape(B, M, MEM_HEADS, HEAD_DIM)
    return k, v


def mem_attend(q, k_mem, v_mem):
    s = jnp.einsum('bthd,bmhd->bhtm', q, k_mem.astype(q.dtype)).astype(jnp.float32) * HEAD_DIM ** -0.5
    p = jax.nn.softmax(s, axis=-1).astype(q.dtype)
    return jnp.einsum('bhtm,bmhd->bthd', p, v_mem.astype(q.dtype))


def conv_ffn(h, prev, w_up, conv_w, conv_b, w_down):
    u = h @ w_up
    ext = jnp.concatenate([prev.astype(u.dtype), u], axis=1)
    c = lax.conv_general_dilated(ext, conv_w[:, None, :].astype(ext.dtype), window_strides=(1,), padding='VALID',
                                 dimension_numbers=('NWC', 'WIO', 'NWC'), feature_group_count=2 * FFN_DIM)
    c = c + conv_b
    g, val = c[..., :FFN_DIM], c[..., FFN_DIM:]
    return (jax.nn.silu(g) * val) @ w_down, ext[:, -CONV_CTX:]


def finish(x, pool_y, attn_y, mem_y, w_out, norm_ffn, conv_prev, w_up, conv_w, conv_b, w_down):
    B, T, _ = x.shape
    mix = jnp.concatenate([pool_y, attn_y, mem_y.reshape(B, T, MEM_WIDTH)], axis=-1)
    x = x + mix @ w_out
    y, conv_state = conv_ffn(rms_norm(x, norm_ffn), conv_prev, w_up, conv_w, conv_b, w_down)
    return x + y, conv_state


def setup_inputs(seed: int = 0) -> dict:
    key = jax.random.key(seed)
    ks = jax.random.split(key, 26)
    f32 = jnp.float32
    n_pages = PAST_LEN // PAGE_SIZE
    n_pool = (5 * DEC_BATCH * n_pages + 3) // 4

    def nrm(k, shape, s=1.0):
        return s * jax.random.normal(k, shape, f32)

    def gain(k, shape):
        return 1.0 + 0.02 * jax.random.normal(k, shape, f32)

    perm = jax.random.permutation(ks[8], n_pool)
    page_table = perm[:DEC_BATCH * n_pages].reshape(DEC_BATCH, n_pages).astype(jnp.int32)
    return {
        "x_prompt": nrm(ks[0], (BATCH, SEQ, D_MODEL)),
        "x_sample": nrm(ks[1], (DEC_BATCH, DEC_SEQ, D_MODEL)),
        "cache_k": nrm(ks[2], (DEPTH, n_pool, PAGE_SIZE, N_HEADS, HEAD_DIM)),
        "cache_v": nrm(ks[3], (DEPTH, n_pool, PAGE_SIZE, N_HEADS, HEAD_DIM)),
        "cache_mem_k": nrm(ks[4], (DEPTH, DEC_BATCH, N_MEM, MEM_HEADS, HEAD_DIM)),
        "cache_mem_v": nrm(ks[5], (DEPTH, DEC_BATCH, N_MEM, MEM_HEADS, HEAD_DIM)),
        "state_pool": nrm(ks[6], (DEPTH, DEC_BATCH, POOL_CTX, POOL_WIDTH)),
        "state_conv": nrm(ks[7], (DEPTH, DEC_BATCH, CONV_CTX, 2 * FFN_DIM)),
        "page_table": page_table,
        "mem_prompt": nrm(ks[9], (BATCH, N_MEM, D_MODEL)),
        "norm_mix": gain(ks[10], (DEPTH, D_MODEL)),
        "w_in": nrm(ks[11], (DEPTH, D_MODEL, IN_WIDTH), D_MODEL ** -0.5),
        "pool_w": nrm(ks[12], (DEPTH, len(POOL_WINDOWS), POOL_GROUP, POOL_GROUP), POOL_GROUP ** -0.5),
        "pool_scale": 1.0 + 0.1 * jax.random.normal(ks[13], (DEPTH, POOL_WIDTH), f32),
        "q_gain": gain(ks[14], (DEPTH, HEAD_DIM)),
        "k_gain": gain(ks[15], (DEPTH, HEAD_DIM)),
        "norm_mem": gain(ks[16], (DEPTH, D_MODEL)),
        "w_mem_kv": nrm(ks[17], (DEPTH, D_MODEL, 2 * MEM_WIDTH), D_MODEL ** -0.5),
        "mq_gain": gain(ks[18], (DEPTH, HEAD_DIM)),
        "mk_gain": gain(ks[19], (DEPTH, HEAD_DIM)),
        "w_out": nrm(ks[20], (DEPTH, MIX_WIDTH, D_MODEL), MIX_WIDTH ** -0.5),
        "norm_ffn": gain(ks[21], (DEPTH, D_MODEL)),
        "w_up": nrm(ks[22], (DEPTH, D_MODEL, 2 * FFN_DIM), D_MODEL ** -0.5),
        "conv_w": nrm(ks[23], (DEPTH, CONV_WIDTH, 2 * FFN_DIM), CONV_WIDTH ** -0.5),
        "conv_b": nrm(ks[24], (DEPTH, 2 * FFN_DIM), 0.01),
        "w_down": nrm(ks[25], (DEPTH, FFN_DIM, D_MODEL), FFN_DIM ** -0.5),
    }


def reference(x_prompt, x_sample, cache_k, cache_v, cache_mem_k, cache_mem_v, state_pool, state_conv, page_table,
              mem_prompt, norm_mix, w_in, pool_w, pool_scale, q_gain, k_gain, norm_mem, w_mem_kv, mq_gain, mk_gain,
              w_out, norm_ffn, w_up, conv_w, conv_b, w_down):
    slopes = alibi_slopes()
    past_len = page_table.shape[1] * PAGE_SIZE
    xp, xs = x_prompt, x_sample
    B, DB = xp.shape[0], xs.shape[0]
    kp_l, vp_l, mkp_l, mvp_l, psp_l, csp_l = [], [], [], [], [], []
    ks_l, vs_l, pss_l, css_l = [], [], [], []
    for l in range(DEPTH):
        u, q, k, v, mq = project(xp, norm_mix[l], w_in[l], q_gain[l], k_gain[l], mq_gain[l])
        pool_y, pool_st = pool_mix(u, jnp.zeros((B, POOL_CTX, POOL_WIDTH), u.dtype), 0, pool_w[l], pool_scale[l])
        attn_y = moba_prompt(q, k, v, slopes)
        mk, mv = mem_kv(mem_prompt, norm_mem[l], w_mem_kv[l], mk_gain[l])
        mem_y = mem_attend(mq, mk, mv)
        xp, conv_st = finish(xp, pool_y, attn_y, mem_y, w_out[l], norm_ffn[l],
                             jnp.zeros((B, CONV_CTX, 2 * FFN_DIM), xp.dtype), w_up[l], conv_w[l], conv_b[l], w_down[l])
        kp_l.append(k); vp_l.append(v); mkp_l.append(mk); mvp_l.append(mv); psp_l.append(pool_st); csp_l.append(conv_st)
        u, q, k, v, mq = project(xs, norm_mix[l], w_in[l], q_gain[l], k_gain[l], mq_gain[l])
        pool_y, pool_st = pool_mix(u, state_pool[l], past_len, pool_w[l], pool_scale[l])
        attn_y = moba_sample(q, k, v, cache_k, cache_v, l, page_table, slopes)
        mem_y = mem_attend(mq, cache_mem_k[l], cache_mem_v[l])
        xs, conv_st = finish(xs, pool_y, attn_y, mem_y, w_out[l], norm_ffn[l], state_conv[l],
                             w_up[l], conv_w[l], conv_b[l], w_down[l])
        ks_l.append(k); vs_l.append(v); pss_l.append(pool_st); css_l.append(conv_st)
    return (xp, xs, jnp.stack(kp_l), jnp.stack(vp_l), jnp.stack(mkp_l), jnp.stack(mvp_l), jnp.stack(psp_l),
            jnp.stack(csp_l), jnp.stack(ks_l), jnp.stack(vs_l), jnp.stack(pss_l), jnp.stack(css_l))
```

```python
import functools

import jax
import jax.numpy as jnp
from jax import lax
from jax.experimental import pallas as pl
from jax.experimental.pallas import tpu as pltpu

HEAD_DIM = 128
POOL_WINDOWS = (2, 4, 8, 16)
POOL_CTX = max(POOL_WINDOWS) - 1
POOL_PAD = 16
MEM_HEADS = 4
MOBA_BLOCK = 256
MOBA_TOPK = 3
PAGE_SIZE = 128
CONV_WIDTH = 3
CONV_CTX = CONV_WIDTH - 1
CONV_PAD = 8
NORM_EPS = 1e-6
NEG_INF = -1e30

VMEM_LIMIT_BYTES = 56 * 1024 * 1024

BF16 = jnp.bfloat16
F32 = jnp.float32


def _params(*semantics):
    return pltpu.CompilerParams(dimension_semantics=semantics, vmem_limit_bytes=VMEM_LIMIT_BYTES)


def _whole_vmem():
    return pl.BlockSpec(memory_space=pltpu.VMEM)


def _row_tile(n, want):
    if n <= want:
        return n
    for t in range(want - want % 8, 0, -8):
        if n % t == 0:
            return t
    raise ValueError((n, want))


def _dot(a, b):
    return jnp.dot(a, b, preferred_element_type=F32)


def _dot_nt(a, b, precision=None):
    return lax.dot_general(a, b, (((1,), (1,)), ((), ())), precision=precision, preferred_element_type=F32)


def _rms_scale(x):
    return lax.rsqrt(jnp.mean(x * x, axis=-1, keepdims=True) + NORM_EPS)


FLAT, HEADS, HEADS_MEAN = "flat", "heads", "heads_mean"
_N_OUTS = {FLAT: 1, HEADS: 2, HEADS_MEAN: 3}


def _norm_proj_kernel(x_ref, nw_ref, w_ref, *rest, segs, n_gain, chunk):
    gain_refs = rest[:n_gain]
    out_refs = list(rest[n_gain:])
    tm = x_ref.shape[0]
    x = x_ref[...]
    xn = (x * _rms_scale(x) * nw_ref[...]).astype(BF16)
    for start, width, gi, mode in segs:
        outs = [out_refs.pop(0) for _ in range(_N_OUTS[mode])]
        heads = width // HEAD_DIM
        for c0 in range(0, width, chunk):
            z = _dot(xn, w_ref[:, start + c0:start + c0 + chunk])
            for h0 in range(0, chunk, HEAD_DIM):
                zh = z[:, h0:h0 + HEAD_DIM]
                if gi is not None:
                    zh = zh * _rms_scale(zh) * gain_refs[gi][...]
                col = slice(c0 + h0, c0 + h0 + HEAD_DIM)
                if mode == FLAT:
                    outs[0][:, col] = zh
                    continue
                outs[0][pl.ds((c0 + h0) // HEAD_DIM, tm, stride=heads), :] = zh
                outs[1][:, col] = zh.astype(BF16)
                if mode == HEADS_MEAN:
                    for r in range(tm // MOBA_BLOCK):
                        outs[2][0, r:r + 1, col] = jnp.mean(zh[r * MOBA_BLOCK:(r + 1) * MOBA_BLOCK], axis=0,
                                                            keepdims=True)


def _norm_proj(x2d, norm_w, w_bf16, gains, segs, tm_want=512):
    n, d = x2d.shape
    tm = _row_tile(n, tm_want)
    chunk = 512
    assert all(w % chunk == 0 for _, w, _, _ in segs)
    in_specs = [pl.BlockSpec((tm, d), lambda i: (i, 0)), _whole_vmem(), _whole_vmem()]
    in_specs += [_whole_vmem() for _ in gains]
    out_specs, out_shape = [], []
    for _, w, _, mode in segs:
        heads = w // HEAD_DIM
        if mode == FLAT:
            out_specs.append(pl.BlockSpec((tm, w), lambda i: (i, 0)))
            out_shape.append(jax.ShapeDtypeStruct((n, w), F32))
            continue
        out_specs.append(pl.BlockSpec((tm * heads, HEAD_DIM), lambda i: (i, 0)))
        out_shape.append(jax.ShapeDtypeStruct((n * heads, HEAD_DIM), F32))
        out_specs.append(pl.BlockSpec((tm, w), lambda i: (i, 0)))
        out_shape.append(jax.ShapeDtypeStruct((n, w), BF16))
        if mode == HEADS_MEAN:
            assert tm % MOBA_BLOCK == 0
            out_specs.append(pl.BlockSpec((1, tm // MOBA_BLOCK, w), lambda i: (i, 0, 0)))
            out_shape.append(jax.ShapeDtypeStruct((n // tm, tm // MOBA_BLOCK, w), F32))
    kern = functools.partial(_norm_proj_kernel, segs=tuple(segs), n_gain=len(gains), chunk=chunk)
    return pl.pallas_call(
        kern, out_shape=out_shape, grid=(n // tm,), in_specs=in_specs, out_specs=out_specs,
        compiler_params=_params("arbitrary"), name="norm_proj",
    )(x2d, norm_w.reshape(1, d), w_bf16, *[g.reshape(1, HEAD_DIM) for g in gains])


def _pool_kernel(u_ref, prev_ref, pw_ref, ps_ref, y_ref, ext_ref, *, tt, pos0):
    j = pl.program_id(1)

    @pl.when(j == 0)
    def _():
        ext_ref[0:POOL_PAD, :] = prev_ref[0]

    @pl.when(j > 0)
    def _():
        ext_ref[0:POOL_PAD, :] = ext_ref[tt:tt + POOL_PAD, :]

    u = u_ref[0]
    ext_ref[POOL_PAD:POOL_PAD + tt, :] = u
    pos = pos0 + j * tt + lax.broadcasted_iota(jnp.int32, (tt, 1), 0)
    for g, w in enumerate(POOL_WINDOWS):
        c = slice(g * HEAD_DIM, (g + 1) * HEAD_DIM)
        ug = u[:, c]
        acc = ug
        for s in range(1, w):
            acc = acc + ext_ref[POOL_PAD - s:POOL_PAD - s + tt, c]
        cnt = jnp.minimum(pos + 1, w).astype(F32)
        diff = acc / cnt - ug
        y_ref[0, :, c] = _dot(diff.astype(BF16), pw_ref[g]) * ps_ref[:, c]


def _pool_mix(u, prev, pos0, pool_w_bf16, pool_scale, tt_want=512):
    b, t, c = u.shape
    assert c == HEAD_DIM * len(POOL_WINDOWS)
    tt = _row_tile(t, tt_want)
    assert tt == t or tt >= POOL_PAD
    kern = functools.partial(_pool_kernel, tt=tt, pos0=pos0)
    return pl.pallas_call(
        kern, out_shape=jax.ShapeDtypeStruct((b, t, c), F32), grid=(b, t // tt),
        in_specs=[pl.BlockSpec((1, tt, c), lambda i, j: (i, j, 0)),
                  pl.BlockSpec((1, POOL_PAD, c), lambda i, j: (i, 0, 0)),
                  _whole_vmem(), _whole_vmem()],
        out_specs=pl.BlockSpec((1, tt, c), lambda i, j: (i, j, 0)),
        scratch_shapes=[pltpu.VMEM((POOL_PAD + tt, c), F32)],
        compiler_params=_params("arbitrary", "arbitrary"), name="pool_mix",
    )(u, prev, pool_w_bf16, pool_scale.reshape(1, c))


def _moba_prompt_kernel(slope_ref, q_ref, k_ref, v_ref, km_ref, o_ref, *, t_len):
    L = MOBA_BLOCK
    nblk = t_len // L
    scale = HEAD_DIM ** -0.5
    slope = slope_ref[0, 0:1, 0:1]
    q = q_ref[0]
    qb = q.astype(BF16)
    kb = k_ref[0]
    vb = v_ref[0]

    pen = [None] * nblk
    n_sel = min(MOBA_TOPK, nblk - 1)
    if n_sel > 0:
        kmean = jnp.concatenate([km_ref[0], jnp.zeros((HEAD_DIM - nblk, HEAD_DIM), F32)], axis=0)
        r0 = L
        gate = _dot_nt(q[r0:], kmean, precision=lax.Precision.HIGHEST)
        rows = t_len - r0
        lane = lax.broadcasted_iota(jnp.int32, (rows, HEAD_DIM), 1)
        n_full = (r0 + lax.broadcasted_iota(jnp.int32, (rows, 1), 0)) // L
        gm = jnp.where(lane < n_full, gate, NEG_INF)
        for j in range(nblk - 1):
            gj = gm[:, j:j + 1]
            beats = (gm > gj) | ((gm == gj) & (lane < j))
            rank = jnp.sum(beats.astype(F32), axis=-1, keepdims=True)
            sel = (rank < n_sel) & (n_full > j)
            pen[j] = jnp.where(sel, 0.0, NEG_INF)

    rel = (lax.broadcasted_iota(jnp.int32, (L, L), 0) - lax.broadcasted_iota(jnp.int32, (L, L), 1))
    causal = rel >= 0
    bias0 = -slope * rel.astype(F32)

    for n in range(nblk):
        qn = qb[n * L:(n + 1) * L]
        s_all = _dot_nt(qn, kb[0:(n + 1) * L])
        tiles = []
        for j in range(n + 1):
            s = s_all[:, j * L:(j + 1) * L] * scale + (bias0 - slope * float((n - j) * L))
            if j == n:
                s = jnp.where(causal, s, NEG_INF)
            else:
                s = s + pen[j][(n - 1) * L:n * L]
            tiles.append(s)
        m = tiles[0]
        for s in tiles[1:]:
            m = jnp.maximum(m, s)
        m = jnp.max(m, axis=-1, keepdims=True)
        acc = jnp.zeros((L, HEAD_DIM), F32)
        l = jnp.zeros((L, 1), F32)
        for j in range(n + 1):
            p = jnp.exp(tiles[j] - m)
            l = l + jnp.sum(p, axis=-1, keepdims=True)
            acc = acc + _dot(p.astype(BF16), vb[j * L:(j + 1) * L])
        o_ref[0, n * L:(n + 1) * L, :] = acc / l


def _moba_prompt(q, k_bf16, v_bf16, kmean, slopes_tile, n_heads):
    b, t, w = q.shape
    assert t % MOBA_BLOCK == 0 and w == n_heads * HEAD_DIM
    nblk = t // MOBA_BLOCK
    spec = pl.BlockSpec((1, t, HEAD_DIM), lambda i, h: (i, 0, h))
    kern = functools.partial(_moba_prompt_kernel, t_len=t)
    return pl.pallas_call(
        kern, out_shape=jax.ShapeDtypeStruct((b, t, w), F32), grid=(b, n_heads),
        in_specs=[pl.BlockSpec((1, 8, HEAD_DIM), lambda i, h: (h, 0, 0)), spec, spec, spec,
                  pl.BlockSpec((1, nblk, HEAD_DIM), lambda i, h: (i, 0, h))],
        out_specs=spec,
        compiler_params=_params("arbitrary", "arbitrary"), name="moba_prompt",
    )(slopes_tile, q, k_bf16, v_bf16, kmean)


def _mem_attn_kernel(q_ref, k_ref, v_ref, o_ref):
    scale = HEAD_DIM ** -0.5
    s = _dot_nt(q_ref[0].astype(BF16), k_ref[0].astype(BF16)) * scale
    m = jnp.max(s, axis=-1, keepdims=True)
    p = jnp.exp(s - m)
    l = jnp.sum(p, axis=-1, keepdims=True)
    o_ref[0] = _dot(p.astype(BF16), v_ref[0].astype(BF16)) / l


def _mem_attend(mq, mk, mv, tq_want=1024):
    b, t, w = mq.shape
    m = mk.shape[1]
    tq = _row_tile(t, tq_want)
    qspec = pl.BlockSpec((1, tq, HEAD_DIM), lambda i, h, j: (i, j, h))
    kspec = pl.BlockSpec((1, m, HEAD_DIM), lambda i, h, j: (i, 0, h))
    return pl.pallas_call(
        _mem_attn_kernel, out_shape=jax.ShapeDtypeStruct((b, t, w), F32),
        grid=(b, w // HEAD_DIM, t // tq), in_specs=[qspec, kspec, kspec], out_specs=qspec,
        compiler_params=_params("arbitrary", "arbitrary", "arbitrary"), name="mem_attend",
    )(mq, mk, mv)


def _out_proj_kernel(x_ref, p_ref, a_ref, m_ref, w_ref, nw_ref, x1_ref, h_ref, mix_ref):
    o1 = p_ref.shape[1]
    o2 = o1 + a_ref.shape[1]
    mix_ref[:, 0:o1] = p_ref[...].astype(BF16)
    mix_ref[:, o1:o2] = a_ref[...].astype(BF16)
    mix_ref[:, o2:] = m_ref[...].astype(BF16)
    x1 = x_ref[...] + _dot(mix_ref[...], w_ref[...])
    x1_ref[...] = x1
    h_ref[...] = (x1 * _rms_scale(x1) * nw_ref[...]).astype(BF16)


def _out_proj(x2d, pool_y, attn_y, mem_y, w_out_bf16, norm_ffn, tm_want=512):
    n, d = x2d.shape
    tm = _row_tile(n, tm_want)
    widths = (pool_y.shape[1], attn_y.shape[1], mem_y.shape[1])
    assert sum(widths) == w_out_bf16.shape[0]
    row = lambda w: pl.BlockSpec((tm, w), lambda i: (i, 0))
    return pl.pallas_call(
        _out_proj_kernel,
        out_shape=[jax.ShapeDtypeStruct((n, d), F32), jax.ShapeDtypeStruct((n, d), BF16)],
        grid=(n // tm,),
        in_specs=[row(d), row(widths[0]), row(widths[1]), row(widths[2]), _whole_vmem(), _whole_vmem()],
        out_specs=[row(d), row(d)],
        scratch_shapes=[pltpu.VMEM((tm, sum(widths)), BF16)],
        compiler_params=_params("arbitrary"), name="out_proj",
    )(x2d, pool_y, attn_y, mem_y, w_out_bf16, norm_ffn.reshape(1, d))


def _ffn_up_kernel(h_ref, wg_ref, wv_ref, cwg_ref, cwv_ref, cbg_ref, cbv_ref, pg_ref, pv_ref,
                   a_ref, sg_ref, sv_ref, eg_ref, ev_ref, *, nseq, tt, fc):
    j = pl.program_id(2)
    lo = CONV_PAD - CONV_CTX
    hm = h_ref[...]

    def conv(w_ref, cw_ref, cb_ref, prev_ref, ext_ref, state_ref):
        u = _dot(hm, w_ref[...]).reshape(nseq, tt, fc)

        @pl.when(j == 0)
        def _():
            ext_ref[:, lo:CONV_PAD, :] = prev_ref[...]

        @pl.when(j > 0)
        def _():
            ext_ref[:, lo:CONV_PAD, :] = ext_ref[:, tt + lo:tt + CONV_PAD, :]

        ext_ref[:, CONV_PAD:CONV_PAD + tt, :] = u
        state_ref[...] = u[:, tt - CONV_CTX:, :]
        out = cb_ref[...].reshape(1, 1, fc)
        for kk in range(CONV_WIDTH):
            out = out + ext_ref[:, lo + kk:lo + kk + tt, :] * cw_ref[kk:kk + 1, :].reshape(1, 1, fc)
        return out

    g = conv(wg_ref, cwg_ref, cbg_ref, pg_ref, eg_ref, sg_ref)
    val = conv(wv_ref, cwv_ref, cbv_ref, pv_ref, ev_ref, sv_ref)
    act = g * (1.0 / (1.0 + jnp.exp(-g))) * val
    a_ref[...] = act.reshape(nseq * tt, fc).astype(BF16)


def _ffn_up(h2d, nseq, tt, w_up_bf16, conv_w, conv_b, prev, fc=512):
    n, d = h2d.shape
    f2 = w_up_bf16.shape[1]
    f = f2 // 2
    assert f % fc == 0
    nfc = f // fc
    n_seqs = prev.shape[0]
    seq_len = n // n_seqs
    assert seq_len % tt == 0 and (nseq == 1 or tt == seq_len) and n_seqs % nseq == 0
    tiles_per_seq = seq_len // tt
    rows = nseq * tt
    kern = functools.partial(_ffn_up_kernel, nseq=nseq, tt=tt, fc=fc)
    wspec = lambda off: pl.BlockSpec((d, fc), lambda i, c, j: (0, c + off))
    vspec = lambda r, off: pl.BlockSpec((r, fc), lambda i, c, j: (0, c + off))
    pspec = lambda off: pl.BlockSpec((nseq, CONV_CTX, fc), lambda i, c, j: (i, 0, c + off))
    sspec = pl.BlockSpec((nseq, CONV_CTX, fc), lambda i, c, j: (i, 0, c))
    return pl.pallas_call(
        kern,
        out_shape=[jax.ShapeDtypeStruct((n, f), BF16),
                   jax.ShapeDtypeStruct((n_seqs, CONV_CTX, f), F32),
                   jax.ShapeDtypeStruct((n_seqs, CONV_CTX, f), F32)],
        grid=(n_seqs // nseq, nfc, tiles_per_seq),
        in_specs=[pl.BlockSpec((rows, d), lambda i, c, j: (i * tiles_per_seq + j, 0)),
                  wspec(0), wspec(nfc), vspec(CONV_WIDTH, 0), vspec(CONV_WIDTH, nfc),
                  vspec(1, 0), vspec(1, nfc), pspec(0), pspec(nfc)],
        out_specs=[pl.BlockSpec((rows, fc), lambda i, c, j: (i * tiles_per_seq + j, c)), sspec, sspec],
        scratch_shapes=[pltpu.VMEM((nseq, CONV_PAD + tt, fc), F32), pltpu.VMEM((nseq, CONV_PAD + tt, fc), F32)],
        compiler_params=_params("arbitrary", "arbitrary", "arbitrary"), name="ffn_up",
    )(h2d, w_up_bf16, w_up_bf16, conv_w, conv_w, conv_b.reshape(1, f2), conv_b.reshape(1, f2), prev, prev)


def _ffn_down_kernel(a_ref, x_ref, w_ref, y_ref):
    y_ref[...] = x_ref[...] + _dot(a_ref[...], w_ref[...])


def _ffn_down(act, x1, w_down_bf16, tm_want=256):
    n, f = act.shape
    d = x1.shape[1]
    tm = _row_tile(n, tm_want)
    return pl.pallas_call(
        _ffn_down_kernel, out_shape=jax.ShapeDtypeStruct((n, d), F32), grid=(n // tm,),
        in_specs=[pl.BlockSpec((tm, f), lambda i: (i, 0)), pl.BlockSpec((tm, d), lambda i: (i, 0)), _whole_vmem()],
        out_specs=pl.BlockSpec((tm, d), lambda i: (i, 0)),
        compiler_params=_params("arbitrary"), name="ffn_down",
    )(act, x1, w_down_bf16)


KMEAN_BLOCKS_PER_STEP = 8


def _kmean_kernel(pt_ref, *refs, n_heads):
    del pt_ref
    page_refs, o_ref = refs[:-1], refs[-1]
    ppb = MOBA_BLOCK // PAGE_SIZE
    for n in range(KMEAN_BLOCKS_PER_STEP):
        s = page_refs[n * ppb][...]
        for p in range(1, ppb):
            s = s + page_refs[n * ppb + p][...]
        o_ref[0, n * n_heads:(n + 1) * n_heads, :] = (
            jnp.sum(s.reshape(PAGE_SIZE, n_heads, HEAD_DIM), axis=0) * (1.0 / MOBA_BLOCK))


def _paged_kmean(cache_flat, page_table, n_full, n_heads):
    db = page_table.shape[0]
    ppb = MOBA_BLOCK // PAGE_SIZE
    nb = KMEAN_BLOCKS_PER_STEP
    assert n_full % nb == 0
    pages_per_step = nb * ppb
    page_rows = PAGE_SIZE * n_heads

    def page_spec(p):
        return pl.BlockSpec((page_rows, HEAD_DIM), lambda b, i, pt: (pt[b, i * pages_per_step + p], 0))

    grid_spec = pltpu.PrefetchScalarGridSpec(
        num_scalar_prefetch=1, grid=(db, n_full // nb),
        in_specs=[page_spec(p) for p in range(pages_per_step)],
        out_specs=pl.BlockSpec((1, nb * n_heads, HEAD_DIM), lambda b, i, pt: (b, i, 0)))
    return pl.pallas_call(
        functools.partial(_kmean_kernel, n_heads=n_heads),
        out_shape=jax.ShapeDtypeStruct((db, n_full * n_heads, HEAD_DIM), F32), grid_spec=grid_spec,
        compiler_params=_params("arbitrary", "arbitrary"), name="paged_kmean",
    )(page_table, *([cache_flat] * pages_per_step))


TOPK_LANES = 4


def _gate_topk_kernel(q_ref, km_ref, idx_ref, *, n_heads, n_sel):
    q = q_ref[0]
    tn = q.shape[0]
    n_full = km_ref.shape[1] // n_heads
    lane = lax.broadcasted_iota(jnp.int32, (tn, n_full), 1).astype(F32)
    out_lane = lax.broadcasted_iota(jnp.int32, (tn, HEAD_DIM), 1)
    out = jnp.zeros((tn, HEAD_DIM), F32)
    for h in range(n_heads):
        c = slice(h * HEAD_DIM, (h + 1) * HEAD_DIM)
        km = km_ref[0, pl.ds(h, n_full, stride=n_heads), :]
        g = _dot_nt(q[:, c], km, precision=lax.Precision.HIGHEST)
        for r in range(n_sel):
            m = jnp.max(g, axis=-1, keepdims=True)
            idx = jnp.min(jnp.where(g == m, lane, float(n_full)), axis=-1, keepdims=True)
            out = jnp.where(out_lane == h * TOPK_LANES + r, idx, out)
            g = jnp.where(lane == idx, -jnp.inf, g)
    idx_ref[0] = out.astype(jnp.int32)


def _gate_topk(q_s, kmean, n_heads, n_sel):
    db, tn, w = q_s.shape
    rows = kmean.shape[1]
    assert n_heads * TOPK_LANES <= HEAD_DIM and n_sel <= TOPK_LANES
    kern = functools.partial(_gate_topk_kernel, n_heads=n_heads, n_sel=n_sel)
    return pl.pallas_call(
        kern, out_shape=jax.ShapeDtypeStruct((db, tn, HEAD_DIM), jnp.int32), grid=(db,),
        in_specs=[pl.BlockSpec((1, tn, w), lambda b: (b, 0, 0)),
                  pl.BlockSpec((1, rows, HEAD_DIM), lambda b: (b, 0, 0))],
        out_specs=pl.BlockSpec((1, tn, HEAD_DIM), lambda b: (b, 0, 0)),
        compiler_params=_params("arbitrary"), name="gate_topk",
    )(q_s, kmean)


def _moba_sample_kernel(pt_ref, fl_ref, slope_ref, idx_ref, q_ref, kn_ref, vn_ref, *rest,
                        tn, n_sel, n_heads, n_full, past):
    del pt_ref
    L = MOBA_BLOCK
    ppb = L // PAGE_SIZE
    k_refs, v_refs = rest[:ppb], rest[ppb:2 * ppb]
    o_ref, m_ref, l_ref, acc_ref = rest[2 * ppb:]
    b = pl.program_id(0)
    n = pl.program_id(1)
    scale = HEAD_DIM ** -0.5

    @pl.when(n == 0)
    def _():
        m_ref[...] = jnp.full(m_ref.shape, NEG_INF, F32)
        l_ref[...] = jnp.zeros(l_ref.shape, F32)
        acc_ref[...] = jnp.zeros(acc_ref.shape, F32)

    def head_cols(h):
        return slice(h * HEAD_DIM, (h + 1) * HEAD_DIM)

    def update(h, s, valid, v_bf16):
        m_old = m_ref[h][:, 0:1]
        m_new = jnp.maximum(m_old, jnp.max(s, axis=-1, keepdims=True))
        alpha = jnp.exp(m_old - m_new)
        p = jnp.exp(s - m_new)
        if valid is not None:
            p = jnp.where(valid, p, 0.0)
        l_new = alpha * l_ref[h][:, 0:1] + jnp.sum(p, axis=-1, keepdims=True)
        acc_new = alpha * acc_ref[h] + _dot(p.astype(BF16), v_bf16)
        m_ref[h] = jnp.broadcast_to(m_new, (tn, HEAD_DIM))
        l_ref[h] = jnp.broadcast_to(l_new, (tn, HEAD_DIM))
        acc_ref[h] = acc_new
        return l_new, acc_new

    flags = fl_ref[b, n]
    row = lax.broadcasted_iota(jnp.int32, (tn, L), 0)
    lane = lax.broadcasted_iota(jnp.int32, (tn, L), 1)
    rel = (past + row - (n * L + lane)).astype(F32)
    idx_lane = lax.broadcasted_iota(jnp.int32, (tn, HEAD_DIM), 1)
    hit = idx_ref[0] == n
    for h in range(n_heads):
        @pl.when(((flags >> h) & 1) == 1)
        def _(h=h):
            slope = slope_ref[h, 0:1, 0:1]
            kh = jnp.concatenate([r[pl.ds(h, PAGE_SIZE, stride=n_heads), :] for r in k_refs], axis=0).astype(BF16)
            vh = jnp.concatenate([r[pl.ds(h, PAGE_SIZE, stride=n_heads), :] for r in v_refs], axis=0).astype(BF16)
            qh = q_ref[0, :, head_cols(h)].astype(BF16)
            s = _dot_nt(qh, kh) * scale - slope * rel
            mine = hit & (idx_lane >= h * TOPK_LANES) & (idx_lane < h * TOPK_LANES + n_sel)
            sel = jnp.max(jnp.where(mine, 1.0, 0.0), axis=-1, keepdims=True) > 0.0
            update(h, jnp.where(sel, s, NEG_INF), sel, vh)

    @pl.when(n == n_full - 1)
    def _():
        row_n = lax.broadcasted_iota(jnp.int32, (tn, tn), 0)
        lane_n = lax.broadcasted_iota(jnp.int32, (tn, tn), 1)
        for h in range(n_heads):
            slope = slope_ref[h, 0:1, 0:1]
            c = head_cols(h)
            s = _dot_nt(q_ref[0, :, c].astype(BF16), kn_ref[0, :, c]) * scale - slope * (row_n - lane_n).astype(F32)
            l, acc = update(h, jnp.where(lane_n <= row_n, s, NEG_INF), None, vn_ref[0, :, c])
            o_ref[0, :, c] = acc / l


def _moba_sample(q_s, k_new_bf16, v_new_bf16, cache_k_flat, cache_v_flat, page_table, idx_lanes, flags, slopes_tile,
                 n_heads, n_sel):
    db, tn, w = q_s.shape
    ppb = MOBA_BLOCK // PAGE_SIZE
    past = page_table.shape[1] * PAGE_SIZE
    n_full = flags.shape[1]
    page_rows = PAGE_SIZE * n_heads
    per_b = lambda width: pl.BlockSpec((1, tn, width), lambda b, n, pt, fl: (b, 0, 0))
    page_specs = [pl.BlockSpec((page_rows, HEAD_DIM), lambda b, n, pt, fl, p=p: (pt[b, n * ppb + p], 0))
                  for p in range(ppb)]
    grid_spec = pltpu.PrefetchScalarGridSpec(
        num_scalar_prefetch=2, grid=(db, n_full),
        in_specs=[_whole_vmem(), per_b(HEAD_DIM), per_b(w), per_b(w), per_b(w)] + page_specs + page_specs,
        out_specs=per_b(w),
        scratch_shapes=[pltpu.VMEM((n_heads, tn, HEAD_DIM), F32)] * 3)
    kern = functools.partial(_moba_sample_kernel, tn=tn, n_sel=n_sel, n_heads=n_heads, n_full=n_full, past=past)
    return pl.pallas_call(
        kern, out_shape=jax.ShapeDtypeStruct((db, tn, w), F32), grid_spec=grid_spec,
        compiler_params=_params("arbitrary", "arbitrary"), name="moba_sample",
    )(page_table, flags, slopes_tile, idx_lanes, q_s, k_new_bf16, v_new_bf16,
      *([cache_k_flat] * ppb), *([cache_v_flat] * ppb))


def _last_rows(prev, new, n):
    t = new.shape[1]
    if t >= n:
        return new[:, t - n:]
    return jnp.concatenate([prev[:, prev.shape[1] - (n - t):], new], axis=1)


def kernel(x_prompt, x_sample, cache_k, cache_v, cache_mem_k, cache_mem_v, state_pool, state_conv, page_table, mem_prompt, norm_mix, w_in, pool_w, pool_scale, q_gain, k_gain, norm_mem, w_mem_kv, mq_gain, mk_gain, w_out, norm_ffn, w_up, conv_w, conv_b, w_down):
    depth = w_in.shape[0]
    assert depth == 1
    bsz, seq, d = x_prompt.shape
    db, tn, _ = x_sample.shape
    pool_width = pool_w.shape[1] * pool_w.shape[2]
    n_heads = cache_k.shape[3]
    attn_width = n_heads * HEAD_DIM
    mem_width = MEM_HEADS * HEAD_DIM
    n_mem = mem_prompt.shape[1]
    n_pool = cache_k.shape[1]
    n_pages = page_table.shape[1]
    past = n_pages * PAGE_SIZE
    n_full = past // MOBA_BLOCK
    assert past % MOBA_BLOCK == 0
    f2 = w_up.shape[2]

    slopes = jnp.exp2(-8.0 * jnp.arange(1, n_heads + 1, dtype=F32) / n_heads)
    slopes_tile = jnp.broadcast_to(slopes[:, None, None], (n_heads, 8, HEAD_DIM))

    l = 0
    w_in_b = w_in[l].astype(BF16)
    w_out_b = w_out[l].astype(BF16)
    w_up_b = w_up[l].astype(BF16)
    w_down_b = w_down[l].astype(BF16)
    w_mem_b = w_mem_kv[l].astype(BF16)
    pool_w_b = pool_w[l].astype(BF16)

    o1 = pool_width
    o2 = o1 + attn_width
    o3 = o2 + attn_width
    o4 = o3 + attn_width
    def in_segs(k_mode):
        return [(0, pool_width, None, FLAT), (o1, attn_width, 0, FLAT), (o2, attn_width, 1, k_mode),
                (o3, attn_width, None, HEADS), (o4, mem_width, 2, FLAT)]

    in_gains = [q_gain[l], k_gain[l], mq_gain[l]]

    def mix_and_ffn(x2d, n_seqs, pool_y, attn_y, mem_y, conv_prev, nseq, tt):
        x1, h = _out_proj(x2d, pool_y, attn_y, mem_y, w_out_b, norm_ffn[l])
        act, sg, sv = _ffn_up(h, nseq, tt, w_up_b, conv_w[l], conv_b[l], conv_prev)
        y = _ffn_down(act, x1, w_down_b)
        return y, jnp.concatenate([sg, sv], axis=-1)

    xp2 = x_prompt.reshape(bsz * seq, d)
    u, q, k, k_b, k_mean, v, v_b, mq = _norm_proj(xp2, norm_mix[l], w_in_b, in_gains, in_segs(HEADS_MEAN))
    u3 = u.reshape(bsz, seq, pool_width)
    pool_y = _pool_mix(u3, jnp.zeros((bsz, POOL_PAD, pool_width), F32), 0, pool_w_b, pool_scale[l])
    attn_y = _moba_prompt(q.reshape(bsz, seq, attn_width), k_b.reshape(bsz, seq, attn_width),
                          v_b.reshape(bsz, seq, attn_width), k_mean.reshape(bsz, seq // MOBA_BLOCK, attn_width),
                          slopes_tile, n_heads)
    mk, mv = _norm_proj(mem_prompt.reshape(bsz * n_mem, d), norm_mem[l], w_mem_b, [mk_gain[l]],
                        [(0, mem_width, 0, FLAT), (mem_width, mem_width, None, FLAT)])
    mem_y = _mem_attend(mq.reshape(bsz, seq, mem_width), mk.reshape(bsz, n_mem, mem_width),
                        mv.reshape(bsz, n_mem, mem_width))
    y_p, conv_st_p = mix_and_ffn(xp2, bsz, pool_y.reshape(bsz * seq, pool_width), attn_y.reshape(bsz * seq, attn_width),
                                 mem_y.reshape(bsz * seq, mem_width), jnp.zeros((bsz, CONV_CTX, f2), F32),
                                 1, _row_tile(seq, 1024))
    pool_st_p = _last_rows(jnp.zeros((bsz, POOL_CTX, pool_width), F32), u3, POOL_CTX)

    xs2 = x_sample.reshape(db * tn, d)
    us, qs, ks, ks_b, vs, vs_b, mqs = _norm_proj(xs2, norm_mix[l], w_in_b, in_gains, in_segs(HEADS))
    us3 = us.reshape(db, tn, pool_width)
    prev_pool = jnp.concatenate([jnp.zeros((db, POOL_PAD - POOL_CTX, pool_width), F32), state_pool[l]], axis=1)
    pool_ys = _pool_mix(us3, prev_pool, past, pool_w_b, pool_scale[l])
    qs3, ks3, vs3 = (a.reshape(db, tn, attn_width) for a in (qs, ks_b, vs_b))
    cache_k_flat = cache_k[l].reshape(n_pool * PAGE_SIZE * n_heads, HEAD_DIM)
    cache_v_flat = cache_v[l].reshape(n_pool * PAGE_SIZE * n_heads, HEAD_DIM)
    n_sel = min(MOBA_TOPK, n_full)
    assert n_sel > 0
    kmean = _paged_kmean(cache_k_flat, page_table, n_full, n_heads)
    idx_lanes = _gate_topk(qs3, kmean, n_heads, n_sel)
    idx = idx_lanes[:, :, :n_heads * TOPK_LANES].reshape(db, tn, n_heads, TOPK_LANES)[..., :n_sel]
    picked = (idx[..., None] == jnp.arange(n_full, dtype=jnp.int32)).any(axis=(1, 3))
    flags = jnp.sum(picked.astype(jnp.int32) << jnp.arange(n_heads, dtype=jnp.int32)[None, :, None], axis=1)
    attn_ys = _moba_sample(qs3, ks3, vs3, cache_k_flat, cache_v_flat, page_table, idx_lanes, flags, slopes_tile,
                           n_heads, n_sel)
    mem_ys = _mem_attend(mqs.reshape(db, tn, mem_width), cache_mem_k[l].reshape(db, n_mem, mem_width),
                         cache_mem_v[l].reshape(db, n_mem, mem_width))
    y_s, conv_st_s = mix_and_ffn(xs2, db, pool_ys.reshape(db * tn, pool_width), attn_ys.reshape(db * tn, attn_width),
                                 mem_ys.reshape(db * tn, mem_width), state_conv[l], db, tn)
    pool_st_s = _last_rows(state_pool[l], us3, POOL_CTX)

    hd = HEAD_DIM
    return (y_p.reshape(bsz, seq, d), y_s.reshape(db, tn, d),
            k.reshape(1, bsz, seq, n_heads, hd), v.reshape(1, bsz, seq, n_heads, hd),
            mk.reshape(1, bsz, n_mem, MEM_HEADS, hd), mv.reshape(1, bsz, n_mem, MEM_HEADS, hd),
            pool_st_p[None], conv_st_p[None],
            ks.reshape(1, db, tn, n_heads, hd), vs.reshape(1, db, tn, n_heads, hd),
            pool_st_s[None], conv_st_s[None])
```

```python
import functools

import jax
import jax.numpy as jnp
from jax import lax
from jax.experimental import pallas as pl
from jax.experimental.pallas import tpu as pltpu

HEAD_DIM = 128
POOL_WINDOWS = (2, 4, 8, 16)
POOL_CTX = max(POOL_WINDOWS) - 1
POOL_PAD = 16
MEM_HEADS = 4
MOBA_BLOCK = 256
MOBA_TOPK = 3
PAGE_SIZE = 128
CONV_WIDTH = 3
CONV_CTX = CONV_WIDTH - 1
CONV_PAD = 8
NORM_EPS = 1e-6
NEG_INF = -1e30

VMEM_LIMIT_BYTES = 56 * 1024 * 1024

BF16 = jnp.bfloat16
F32 = jnp.float32


def _params(*semantics):
    return pltpu.CompilerParams(dimension_semantics=semantics, vmem_limit_bytes=VMEM_LIMIT_BYTES)


def _whole_vmem():
    return pl.BlockSpec(memory_space=pltpu.VMEM)


def _row_tile(n, want):
    if n <= want:
        return n
    for t in range(want - want % 8, 0, -8):
        if n % t == 0:
            return t
    raise ValueError((n, want))


def _dot(a, b):
    return jnp.dot(a, b, preferred_element_type=F32)


def _dot_nt(a, b, precision=None):
    return lax.dot_general(a, b, (((1,), (1,)), ((), ())), precision=precision, preferred_element_type=F32)


def _rms_scale(x):
    return lax.rsqrt(jnp.mean(x * x, axis=-1, keepdims=True) + NORM_EPS)


FLAT, HEADS, HEADS_MEAN = "flat", "heads", "heads_mean"
_N_OUTS = {FLAT: 1, HEADS: 2, HEADS_MEAN: 3}


def _norm_proj_kernel(x_ref, nw_ref, w_ref, *rest, segs, n_gain, chunk):
    gain_refs = rest[:n_gain]
    out_refs = list(rest[n_gain:])
    tm = x_ref.shape[0]
    x = x_ref[...]
    xn = (x * _rms_scale(x) * nw_ref[...]).astype(BF16)
    for start, width, gi, mode in segs:
        outs = [out_refs.pop(0) for _ in range(_N_OUTS[mode])]
        heads = width // HEAD_DIM
        for c0 in range(0, width, chunk):
            z = _dot(xn, w_ref[:, start + c0:start + c0 + chunk])
            for h0 in range(0, chunk, HEAD_DIM):
                zh = z[:, h0:h0 + HEAD_DIM]
                if gi is not None:
                    zh = zh * _rms_scale(zh) * gain_refs[gi][...]
                col = slice(c0 + h0, c0 + h0 + HEAD_DIM)
                if mode == FLAT:
                    outs[0][:, col] = zh
                    continue
                outs[0][pl.ds((c0 + h0) // HEAD_DIM, tm, stride=heads), :] = zh
                outs[1][:, col] = zh.astype(BF16)
                if mode == HEADS_MEAN:
                    for r in range(tm // MOBA_BLOCK):
                        outs[2][0, r:r + 1, col] = jnp.mean(zh[r * MOBA_BLOCK:(r + 1) * MOBA_BLOCK], axis=0,
                                                            keepdims=True)


def _norm_proj(x2d, norm_w, w_bf16, gains, segs, tm_want=512):
    n, d = x2d.shape
    tm = _row_tile(n, tm_want)
    chunk = 512
    assert all(w % chunk == 0 for _, w, _, _ in segs)
    in_specs = [pl.BlockSpec((tm, d), lambda i: (i, 0)), _whole_vmem(), _whole_vmem()]
    in_specs += [_whole_vmem() for _ in gains]
    out_specs, out_shape = [], []
    for _, w, _, mode in segs:
        heads = w // HEAD_DIM
        if mode == FLAT:
            out_specs.append(pl.BlockSpec((tm, w), lambda i: (i, 0)))
            out_shape.append(jax.ShapeDtypeStruct((n, w), F32))
            continue
        out_specs.append(pl.BlockSpec((tm * heads, HEAD_DIM), lambda i: (i, 0)))
        out_shape.append(jax.ShapeDtypeStruct((n * heads, HEAD_DIM), F32))
        out_specs.append(pl.BlockSpec((tm, w), lambda i: (i, 0)))
        out_shape.append(jax.ShapeDtypeStruct((n, w), BF16))
        if mode == HEADS_MEAN:
            assert tm % MOBA_BLOCK == 0
            out_specs.append(pl.BlockSpec((1, tm // MOBA_BLOCK, w), lambda i: (i, 0, 0)))
            out_shape.append(jax.ShapeDtypeStruct((n // tm, tm // MOBA_BLOCK, w), F32))
    kern = functools.partial(_norm_proj_kernel, segs=tuple(segs), n_gain=len(gains), chunk=chunk)
    return pl.pallas_call(
        kern, out_shape=out_shape, grid=(n // tm,), in_specs=in_specs, out_specs=out_specs,
        compiler_params=_params("arbitrary"), name="norm_proj",
    )(x2d, norm_w.reshape(1, d), w_bf16, *[g.reshape(1, HEAD_DIM) for g in gains])


def _pool_kernel(u_ref, prev_ref, pw_ref, ps_ref, y_ref, ext_ref, *, tt, pos0):
    j = pl.program_id(1)

    @pl.when(j == 0)
    def _():
        ext_ref[0:POOL_PAD, :] = prev_ref[0]

    @pl.when(j > 0)
    def _():
        ext_ref[0:POOL_PAD, :] = ext_ref[tt:tt + POOL_PAD, :]

    u = u_ref[0]
    ext_ref[POOL_PAD:POOL_PAD + tt, :] = u
    pos = pos0 + j * tt + lax.broadcasted_iota(jnp.int32, (tt, 1), 0)
    for g, w in enumerate(POOL_WINDOWS):
        c = slice(g * HEAD_DIM, (g + 1) * HEAD_DIM)
        ug = u[:, c]
        acc = ug
        for s in range(1, w):
            acc = acc + ext_ref[POOL_PAD - s:POOL_PAD - s + tt, c]
        cnt = jnp.minimum(pos + 1, w).astype(F32)
        diff = acc / cnt - ug
        y_ref[0, :, c] = _dot(diff.astype(BF16), pw_ref[g]) * ps_ref[:, c]


def _pool_mix(u, prev, pos0, pool_w_bf16, pool_scale, tt_want=512):
    b, t, c = u.shape
    assert c == HEAD_DIM * len(POOL_WINDOWS)
    tt = _row_tile(t, tt_want)
    assert tt == t or tt >= POOL_PAD
    kern = functools.partial(_pool_kernel, tt=tt, pos0=pos0)
    return pl.pallas_call(
        kern, out_shape=jax.ShapeDtypeStruct((b, t, c), F32), grid=(b, t // tt),
        in_specs=[pl.BlockSpec((1, tt, c), lambda i, j: (i, j, 0)),
                  pl.BlockSpec((1, POOL_PAD, c), lambda i, j: (i, 0, 0)),
                  _whole_vmem(), _whole_vmem()],
        out_specs=pl.BlockSpec((1, tt, c), lambda i, j: (i, j, 0)),
        scratch_shapes=[pltpu.VMEM((POOL_PAD + tt, c), F32)],
        compiler_params=_params("arbitrary", "arbitrary"), name="pool_mix",
    )(u, prev, pool_w_bf16, pool_scale.reshape(1, c))


def _moba_prompt_kernel(slope_ref, q_ref, k_ref, v_ref, km_ref, o_ref, *, t_len):
    L = MOBA_BLOCK
    nblk = t_len // L
    scale = HEAD_DIM ** -0.5
    slope = slope_ref[0, 0:1, 0:1]
    q = q_ref[0]
    kb = k_ref[0]
    vb = v_ref[0]
    n_sel = min(MOBA_TOPK, nblk - 1)
    assert nblk <= 8 and nblk + 3 <= HEAD_DIM
    F_LO, F_HI, F_ONE = nblk, nblk + 1, nblk + 2
    row = lax.broadcasted_iota(jnp.int32, (t_len, HEAD_DIM), 0)
    lane = lax.broadcasted_iota(jnp.int32, (t_len, HEAD_DIM), 1)
    blk = row // L
    blk_f = blk.astype(F32)
    k_feat = jnp.where(lane == blk, 1.0, 0.0)
    k_feat = jnp.where(lane == F_LO, (row - blk * L).astype(F32), k_feat)
    k_feat = jnp.where(lane == F_HI, blk_f, k_feat)
    k_feat = jnp.where(lane == F_ONE, 1.0, k_feat)
    k_aug = jnp.concatenate([kb, k_feat.astype(BF16)], axis=-1)

    q_feat = jnp.zeros((t_len, HEAD_DIM), F32)
    if n_sel > 0:
        km = jnp.concatenate([km_ref[0], jnp.zeros((HEAD_DIM - nblk, HEAD_DIM), F32)], axis=0)
        km_hi = km.astype(BF16)
        km_lo = (km - km_hi.astype(F32)).astype(BF16)
        r0 = L
        qr = q[r0:]
        q_hi = qr.astype(BF16)
        q_lo = (qr - q_hi.astype(F32)).astype(BF16)
        gate = (_dot_nt(jnp.concatenate([q_hi, q_lo], axis=-1), jnp.concatenate([km_hi, km_hi], axis=-1))
                + _dot_nt(q_hi, km_lo))
        g = gate.T[0:8]
        sub = lax.broadcasted_iota(jnp.int32, g.shape, 0)
        n_full = (r0 + lax.broadcasted_iota(jnp.int32, g.shape, 1)) // L
        gm = jnp.where(sub < n_full, g, NEG_INF)
        pen = jnp.zeros(g.shape, F32)
        for j in range(nblk - 1):
            gj = gm[j:j + 1, :]
            beats = (gm > gj) | ((gm == gj) & (sub < j))
            rank = jnp.sum(beats.astype(F32), axis=0, keepdims=True)
            sel = (rank < n_sel) & (n_full > j)
            pen = jnp.where(sub == j, jnp.where(sel, 0.0, NEG_INF), pen)
        pen = jnp.where(sub < n_full, pen, 0.0)
        pen_t = jnp.concatenate([pen, jnp.zeros((HEAD_DIM - 8, pen.shape[1]), F32)], axis=0).T
        q_feat = jnp.concatenate([jnp.zeros((r0, HEAD_DIM), F32), pen_t], axis=0)
    q_feat = jnp.where(lane == F_LO, slope, q_feat)
    q_feat = jnp.where(lane == F_HI, slope * float(L), q_feat)
    q_feat = jnp.where(lane == F_ONE, -slope * float(L) * blk_f, q_feat)
    q_aug = jnp.concatenate([(q * scale).astype(BF16), q_feat.astype(BF16)], axis=-1)

    causal = (lax.broadcasted_iota(jnp.int32, (L, L), 0) >= lax.broadcasted_iota(jnp.int32, (L, L), 1))
    for n in range(nblk):
        s = _dot_nt(q_aug[n * L:(n + 1) * L], k_aug[0:(n + 1) * L])
        s_own = jnp.where(causal, s[:, n * L:], NEG_INF)
        m = jnp.max(s_own, axis=-1, keepdims=True)
        if n > 0:
            s_past = s[:, :n * L]
            m = jnp.maximum(m, jnp.max(s_past, axis=-1, keepdims=True))
        p_own = jnp.exp(s_own - m)
        l = jnp.sum(p_own, axis=-1, keepdims=True)
        acc = _dot(p_own.astype(BF16), vb[n * L:(n + 1) * L])
        if n > 0:
            p_past = jnp.exp(s_past - m)
            l = l + jnp.sum(p_past, axis=-1, keepdims=True)
            acc = acc + _dot(p_past.astype(BF16), vb[:n * L])
        o_ref[0, n * L:(n + 1) * L, :] = acc / l


def _moba_prompt(q, k_bf16, v_bf16, kmean, slopes_tile, n_heads):
    b, t, w = q.shape
    assert t % MOBA_BLOCK == 0 and w == n_heads * HEAD_DIM
    nblk = t // MOBA_BLOCK
    spec = pl.BlockSpec((1, t, HEAD_DIM), lambda i, h: (i, 0, h))
    kern = functools.partial(_moba_prompt_kernel, t_len=t)
    return pl.pallas_call(
        kern, out_shape=jax.ShapeDtypeStruct((b, t, w), F32), grid=(b, n_heads),
        in_specs=[pl.BlockSpec((1, 8, HEAD_DIM), lambda i, h: (h, 0, 0)), spec, spec, spec,
                  pl.BlockSpec((1, nblk, HEAD_DIM), lambda i, h: (i, 0, h))],
        out_specs=spec,
        compiler_params=_params("arbitrary", "arbitrary"), name="moba_prompt",
    )(slopes_tile, q, k_bf16, v_bf16, kmean)


def _mem_attn_kernel(q_ref, k_ref, v_ref, o_ref):
    scale = HEAD_DIM ** -0.5
    s = _dot_nt(q_ref[0].astype(BF16), k_ref[0].astype(BF16)) * scale
    m = jnp.max(s, axis=-1, keepdims=True)
    p = jnp.exp(s - m)
    l = jnp.sum(p, axis=-1, keepdims=True)
    o_ref[0] = _dot(p.astype(BF16), v_ref[0].astype(BF16)) / l


def _mem_attend(mq, mk, mv, tq_want=1024):
    b, t, w = mq.shape
    m = mk.shape[1]
    tq = _row_tile(t, tq_want)
    qspec = pl.BlockSpec((1, tq, HEAD_DIM), lambda i, h, j: (i, j, h))
    kspec = pl.BlockSpec((1, m, HEAD_DIM), lambda i, h, j: (i, 0, h))
    return pl.pallas_call(
        _mem_attn_kernel, out_shape=jax.ShapeDtypeStruct((b, t, w), F32),
        grid=(b, w // HEAD_DIM, t // tq), in_specs=[qspec, kspec, kspec], out_specs=qspec,
        compiler_params=_params("arbitrary", "arbitrary", "arbitrary"), name="mem_attend",
    )(mq, mk, mv)


def _out_proj_kernel(x_ref, p_ref, a_ref, m_ref, w_ref, nw_ref, x1_ref, h_ref, mix_ref):
    o1 = p_ref.shape[1]
    o2 = o1 + a_ref.shape[1]
    mix_ref[:, 0:o1] = p_ref[...].astype(BF16)
    mix_ref[:, o1:o2] = a_ref[...].astype(BF16)
    mix_ref[:, o2:] = m_ref[...].astype(BF16)
    x1 = x_ref[...] + _dot(mix_ref[...], w_ref[...])
    x1_ref[...] = x1
    h_ref[...] = (x1 * _rms_scale(x1) * nw_ref[...]).astype(BF16)


def _out_proj(x2d, pool_y, attn_y, mem_y, w_out_bf16, norm_ffn, tm_want=512):
    n, d = x2d.shape
    tm = _row_tile(n, tm_want)
    widths = (pool_y.shape[1], attn_y.shape[1], mem_y.shape[1])
    assert sum(widths) == w_out_bf16.shape[0]
    row = lambda w: pl.BlockSpec((tm, w), lambda i: (i, 0))
    return pl.pallas_call(
        _out_proj_kernel,
        out_shape=[jax.ShapeDtypeStruct((n, d), F32), jax.ShapeDtypeStruct((n, d), BF16)],
        grid=(n // tm,),
        in_specs=[row(d), row(widths[0]), row(widths[1]), row(widths[2]), _whole_vmem(), _whole_vmem()],
        out_specs=[row(d), row(d)],
        scratch_shapes=[pltpu.VMEM((tm, sum(widths)), BF16)],
        compiler_params=_params("arbitrary"), name="out_proj",
    )(x2d, pool_y, attn_y, mem_y, w_out_bf16, norm_ffn.reshape(1, d))


def _ffn_up_kernel(h_ref, wg_ref, wv_ref, cwg_ref, cwv_ref, cbg_ref, cbv_ref, pg_ref, pv_ref,
                   a_ref, sg_ref, sv_ref, eg_ref, ev_ref, *, nseq, tt, fc, sub):
    j = pl.program_id(2)
    lo = CONV_PAD - CONV_CTX

    @pl.when(j == 0)
    def _():
        eg_ref[:, lo:CONV_PAD, :] = pg_ref[...]
        ev_ref[:, lo:CONV_PAD, :] = pv_ref[...]

    @pl.when(j > 0)
    def _():
        eg_ref[:, lo:CONV_PAD, :] = eg_ref[:, tt + lo:tt + CONV_PAD, :]
        ev_ref[:, lo:CONV_PAD, :] = ev_ref[:, tt + lo:tt + CONV_PAD, :]

    def conv(hm, r0, w_ref, cw_ref, cb_ref, ext_ref, state_ref):
        u = _dot(hm, w_ref[...]).reshape(nseq, sub, fc)
        ext_ref[:, CONV_PAD + r0:CONV_PAD + r0 + sub, :] = u
        if r0 + sub == tt:
            state_ref[...] = u[:, sub - CONV_CTX:, :]
        out = cb_ref[...].reshape(1, 1, fc)
        for kk in range(CONV_WIDTH):
            out = out + ext_ref[:, lo + r0 + kk:lo + r0 + kk + sub, :] * cw_ref[kk:kk + 1, :].reshape(1, 1, fc)
        return out

    for r0 in range(0, tt, sub):
        rows = slice(r0 * nseq, (r0 + sub) * nseq)
        hm = h_ref[rows, :]
        g = conv(hm, r0, wg_ref, cwg_ref, cbg_ref, eg_ref, sg_ref)
        val = conv(hm, r0, wv_ref, cwv_ref, cbv_ref, ev_ref, sv_ref)
        act = g * (1.0 / (1.0 + jnp.exp(-g))) * val
        a_ref[rows, :] = act.reshape(nseq * sub, fc).astype(BF16)


def _ffn_up_scan_kernel(pt_ref, *refs, n_scan, n_heads, **kw):
    del pt_ref
    n_ffn_in = 9
    page_refs = refs[n_ffn_in:n_ffn_in + n_scan]
    km_ref = refs[n_ffn_in + n_scan + 3]
    ppb = MOBA_BLOCK // PAGE_SIZE
    for n in range(n_scan // ppb):
        s = page_refs[n * ppb][...]
        for p in range(1, ppb):
            s = s + page_refs[n * ppb + p][...]
        km_ref[n * n_heads:(n + 1) * n_heads, :] = (
            jnp.sum(s.reshape(PAGE_SIZE, n_heads, HEAD_DIM), axis=0) * (1.0 / MOBA_BLOCK))
    _ffn_up_kernel(*refs[:n_ffn_in], *refs[n_ffn_in + n_scan:n_ffn_in + n_scan + 3], *refs[n_ffn_in + n_scan + 4:],
                   **kw)


def _ffn_up(h2d, nseq, tt, w_up_bf16, conv_w, conv_b, prev, fc=512, scan=None):
    n, d = h2d.shape
    f2 = w_up_bf16.shape[1]
    f = f2 // 2
    assert f % fc == 0
    nfc = f // fc
    n_seqs = prev.shape[0]
    seq_len = n // n_seqs
    assert seq_len % tt == 0 and (nseq == 1 or tt == seq_len) and n_seqs % nseq == 0
    tiles_per_seq = seq_len // tt
    rows = nseq * tt
    sub = _row_tile(tt, 256) if nseq == 1 else tt
    grid = (n_seqs // nseq, nfc, tiles_per_seq)
    kw = dict(nseq=nseq, tt=tt, fc=fc, sub=sub)
    wspec = lambda off: pl.BlockSpec((d, fc), lambda i, c, j, *_: (0, c + off))
    vspec = lambda r, off: pl.BlockSpec((r, fc), lambda i, c, j, *_: (0, c + off))
    pspec = lambda off: pl.BlockSpec((nseq, CONV_CTX, fc), lambda i, c, j, *_: (i, 0, c + off))
    sspec = pl.BlockSpec((nseq, CONV_CTX, fc), lambda i, c, j, *_: (i, 0, c))
    in_specs = [pl.BlockSpec((rows, d), lambda i, c, j, *_: (i * tiles_per_seq + j, 0)),
                wspec(0), wspec(nfc), vspec(CONV_WIDTH, 0), vspec(CONV_WIDTH, nfc),
                vspec(1, 0), vspec(1, nfc), pspec(0), pspec(nfc)]
    out_specs = [pl.BlockSpec((rows, fc), lambda i, c, j, *_: (i * tiles_per_seq + j, c)), sspec, sspec]
    out_shape = [jax.ShapeDtypeStruct((n, f), BF16),
                 jax.ShapeDtypeStruct((n_seqs, CONV_CTX, f), F32),
                 jax.ShapeDtypeStruct((n_seqs, CONV_CTX, f), F32)]
    scratch = [pltpu.VMEM((nseq, CONV_PAD + tt, fc), F32), pltpu.VMEM((nseq, CONV_PAD + tt, fc), F32)]
    args = (h2d, w_up_bf16, w_up_bf16, conv_w, conv_w, conv_b.reshape(1, f2), conv_b.reshape(1, f2), prev, prev)
    if scan is None:
        return pl.pallas_call(
            functools.partial(_ffn_up_kernel, **kw), out_shape=out_shape, grid=grid, in_specs=in_specs,
            out_specs=out_specs, scratch_shapes=scratch,
            compiler_params=_params("arbitrary", "arbitrary", "arbitrary"), name="ffn_up")(*args)

    cache_flat, pages, n_heads = scan
    ppb = MOBA_BLOCK // PAGE_SIZE
    n_steps = grid[0] * grid[1] * grid[2]
    total = pages.shape[0]
    assert total % ppb == 0
    n_scan = -(-total // (n_steps * ppb)) * ppb
    page_rows = PAGE_SIZE * n_heads
    km_rows = n_scan // ppb * n_heads

    def step_of(i, c, j):
        return (i * nfc + c) * tiles_per_seq + j

    def page_spec(p):
        return pl.BlockSpec((page_rows, HEAD_DIM),
                            lambda i, c, j, pt: (pt[jnp.minimum(step_of(i, c, j) * n_scan + p, total - 1)], 0))

    grid_spec = pltpu.PrefetchScalarGridSpec(
        num_scalar_prefetch=1, grid=grid,
        in_specs=in_specs + [page_spec(p) for p in range(n_scan)],
        out_specs=out_specs + [pl.BlockSpec((km_rows, HEAD_DIM), lambda i, c, j, pt: (step_of(i, c, j), 0))],
        scratch_shapes=scratch)
    act, sg, sv, km = pl.pallas_call(
        functools.partial(_ffn_up_scan_kernel, n_scan=n_scan, n_heads=n_heads, **kw),
        out_shape=out_shape + [jax.ShapeDtypeStruct((n_steps * km_rows, HEAD_DIM), F32)], grid_spec=grid_spec,
        compiler_params=_params("arbitrary", "arbitrary", "arbitrary"), name="ffn_up_scan",
    )(pages, *args, *([cache_flat] * n_scan))
    return act, sg, sv, km[:total // ppb * n_heads]


def _ffn_down_kernel(a_ref, x_ref, w_ref, y_ref):
    y_ref[...] = x_ref[...] + _dot(a_ref[...], w_ref[...])


def _ffn_down(act, x1, w_down_bf16, tm_want=256):
    n, f = act.shape
    d = x1.shape[1]
    tm = _row_tile(n, tm_want)
    return pl.pallas_call(
        _ffn_down_kernel, out_shape=jax.ShapeDtypeStruct((n, d), F32), grid=(n // tm,),
        in_specs=[pl.BlockSpec((tm, f), lambda i: (i, 0)), pl.BlockSpec((tm, d), lambda i: (i, 0)), _whole_vmem()],
        out_specs=pl.BlockSpec((tm, d), lambda i: (i, 0)),
        compiler_params=_params("arbitrary"), name="ffn_down",
    )(act, x1, w_down_bf16)


TOPK_LANES = 4


def _gate_topk_kernel(q_ref, km_ref, idx_ref, *, n_heads, n_sel):
    q = q_ref[0]
    tn = q.shape[0]
    n_full = km_ref.shape[1] // n_heads
    lane = lax.broadcasted_iota(jnp.int32, (tn, n_full), 1).astype(F32)
    out_lane = lax.broadcasted_iota(jnp.int32, (tn, HEAD_DIM), 1)
    out = jnp.zeros((tn, HEAD_DIM), F32)
    for h in range(n_heads):
        c = slice(h * HEAD_DIM, (h + 1) * HEAD_DIM)
        km = km_ref[0, pl.ds(h, n_full, stride=n_heads), :]
        g = _dot_nt(q[:, c], km, precision=lax.Precision.HIGHEST)
        for r in range(n_sel):
            m = jnp.max(g, axis=-1, keepdims=True)
            idx = jnp.min(jnp.where(g == m, lane, float(n_full)), axis=-1, keepdims=True)
            out = jnp.where(out_lane == h * TOPK_LANES + r, idx, out)
            g = jnp.where(lane == idx, -jnp.inf, g)
    idx_ref[0] = out.astype(jnp.int32)


def _gate_topk(q_s, kmean, n_heads, n_sel):
    db, tn, w = q_s.shape
    rows = kmean.shape[1]
    assert n_heads * TOPK_LANES <= HEAD_DIM and n_sel <= TOPK_LANES
    kern = functools.partial(_gate_topk_kernel, n_heads=n_heads, n_sel=n_sel)
    return pl.pallas_call(
        kern, out_shape=jax.ShapeDtypeStruct((db, tn, HEAD_DIM), jnp.int32), grid=(db,),
        in_specs=[pl.BlockSpec((1, tn, w), lambda b: (b, 0, 0)),
                  pl.BlockSpec((1, rows, HEAD_DIM), lambda b: (b, 0, 0))],
        out_specs=pl.BlockSpec((1, tn, HEAD_DIM), lambda b: (b, 0, 0)),
        compiler_params=_params("arbitrary"), name="gate_topk",
    )(q_s, kmean)


def _moba_sample_kernel(pt_ref, idx_ref, slope_ref, q_ref, kn_ref, vn_ref, ck_ref, cv_ref, o_ref,
                        kbuf, vbuf, sem, *, tn, n_sel, n_heads, past):
    L = MOBA_BLOCK
    ppb = L // PAGE_SIZE
    n_slots = tn * n_sel * ppb
    b = pl.program_id(0)
    h = pl.program_id(1)
    step = b * n_heads + h
    n_steps = pl.num_programs(0) * n_heads
    cur = lax.rem(step, 2)
    scale = HEAD_DIM ** -0.5

    def block_of(bb, hh, t, r):
        return idx_ref[((bb * n_heads + hh) * tn + t) * n_sel + r]

    def gathers(bb, hh, buf_slot):
        out = []
        for t in range(tn):
            for r in range(n_sel):
                blk = block_of(bb, hh, t, r)
                for p in range(ppb):
                    page = pt_ref[bb, blk * ppb + p]
                    i = (t * n_sel + r) * ppb + p
                    out.append(pltpu.make_async_copy(ck_ref.at[page, :, hh, :], kbuf.at[buf_slot, i],
                                                     sem.at[0, buf_slot]))
                    out.append(pltpu.make_async_copy(cv_ref.at[page, :, hh, :], vbuf.at[buf_slot, i],
                                                     sem.at[1, buf_slot]))
        return out

    @pl.when(step == 0)
    def _():
        for c in gathers(b, h, cur):
            c.start()

    @pl.when(step + 1 < n_steps)
    def _():
        nxt = step + 1
        for c in gathers(nxt // n_heads, lax.rem(nxt, n_heads), 1 - cur):
            c.start()

    for c in gathers(b, h, cur):
        c.wait()

    slope = slope_ref[0, 0:1, 0:1]
    qb = q_ref[0].astype(BF16)
    n_keys = n_slots * PAGE_SIZE
    s = _dot_nt(qb, kbuf[cur].reshape(n_keys, HEAD_DIM).astype(BF16)) * scale
    lane = lax.broadcasted_iota(jnp.int32, (tn, PAGE_SIZE), 1)
    k_pos = jnp.concatenate(
        [block_of(b, h, t, r) * L + p * PAGE_SIZE + lane for t in range(tn) for r in range(n_sel) for p in range(ppb)],
        axis=1)
    row = lax.broadcasted_iota(jnp.int32, (tn, n_keys), 0)
    col = lax.broadcasted_iota(jnp.int32, (tn, n_keys), 1)
    per_query = n_sel * L
    mine = (col >= row * per_query) & (col < row * per_query + per_query)
    s = jnp.where(mine, s - slope * (past + row - k_pos).astype(F32), NEG_INF)
    row_n = lax.broadcasted_iota(jnp.int32, (tn, tn), 0)
    lane_n = lax.broadcasted_iota(jnp.int32, (tn, tn), 1)
    s_own = _dot_nt(qb, kn_ref[0]) * scale - slope * (row_n - lane_n).astype(F32)
    s_own = jnp.where(lane_n <= row_n, s_own, NEG_INF)
    m = jnp.maximum(jnp.max(s, axis=-1, keepdims=True), jnp.max(s_own, axis=-1, keepdims=True))
    p = jnp.exp(s - m)
    p_own = jnp.exp(s_own - m)
    l = jnp.sum(p, axis=-1, keepdims=True) + jnp.sum(p_own, axis=-1, keepdims=True)
    acc = _dot(p.astype(BF16), vbuf[cur].reshape(n_keys, HEAD_DIM).astype(BF16)) + _dot(p_own.astype(BF16), vn_ref[0])
    o_ref[0] = acc / l


def _moba_sample(q_s, k_new_bf16, v_new_bf16, cache_k4, cache_v4, page_table, idx, slopes_tile, n_heads):
    db, tn, w = q_s.shape
    n_sel = idx.shape[-1]
    ppb = MOBA_BLOCK // PAGE_SIZE
    past = page_table.shape[1] * PAGE_SIZE
    n_slots = tn * n_sel * ppb
    new_spec = pl.BlockSpec((1, tn, HEAD_DIM), lambda b, h, pt, ix: (b, 0, h))
    hbm = pl.BlockSpec(memory_space=pl.ANY)
    grid_spec = pltpu.PrefetchScalarGridSpec(
        num_scalar_prefetch=2, grid=(db, n_heads),
        in_specs=[pl.BlockSpec((1, 8, HEAD_DIM), lambda b, h, pt, ix: (h, 0, 0)), new_spec, new_spec, new_spec,
                  hbm, hbm],
        out_specs=new_spec,
        scratch_shapes=[pltpu.VMEM((2, n_slots, PAGE_SIZE, HEAD_DIM), F32),
                        pltpu.VMEM((2, n_slots, PAGE_SIZE, HEAD_DIM), F32),
                        pltpu.SemaphoreType.DMA((2, 2))])
    kern = functools.partial(_moba_sample_kernel, tn=tn, n_sel=n_sel, n_heads=n_heads, past=past)
    return pl.pallas_call(
        kern, out_shape=jax.ShapeDtypeStruct((db, tn, w), F32), grid_spec=grid_spec,
        compiler_params=_params("arbitrary", "arbitrary"), name="moba_sample",
    )(page_table, idx.reshape(-1), slopes_tile, q_s, k_new_bf16, v_new_bf16, cache_k4, cache_v4)


def _last_rows(prev, new, n):
    t = new.shape[1]
    if t >= n:
        return new[:, t - n:]
    return jnp.concatenate([prev[:, prev.shape[1] - (n - t):], new], axis=1)


def kernel(x_prompt, x_sample, cache_k, cache_v, cache_mem_k, cache_mem_v, state_pool, state_conv, page_table, mem_prompt, norm_mix, w_in, pool_w, pool_scale, q_gain, k_gain, norm_mem, w_mem_kv, mq_gain, mk_gain, w_out, norm_ffn, w_up, conv_w, conv_b, w_down):
    depth = w_in.shape[0]
    assert depth == 1
    bsz, seq, d = x_prompt.shape
    db, tn, _ = x_sample.shape
    pool_width = pool_w.shape[1] * pool_w.shape[2]
    n_heads = cache_k.shape[3]
    attn_width = n_heads * HEAD_DIM
    mem_width = MEM_HEADS * HEAD_DIM
    n_mem = mem_prompt.shape[1]
    n_pool = cache_k.shape[1]
    n_pages = page_table.shape[1]
    past = n_pages * PAGE_SIZE
    n_full = past // MOBA_BLOCK
    assert past % MOBA_BLOCK == 0
    f2 = w_up.shape[2]

    slopes = jnp.exp2(-8.0 * jnp.arange(1, n_heads + 1, dtype=F32) / n_heads)
    slopes_tile = jnp.broadcast_to(slopes[:, None, None], (n_heads, 8, HEAD_DIM))

    l = 0
    w_in_b = w_in[l].astype(BF16)
    w_out_b = w_out[l].astype(BF16)
    w_up_b = w_up[l].astype(BF16)
    w_down_b = w_down[l].astype(BF16)
    w_mem_b = w_mem_kv[l].astype(BF16)
    pool_w_b = pool_w[l].astype(BF16)

    o1 = pool_width
    o2 = o1 + attn_width
    o3 = o2 + attn_width
    o4 = o3 + attn_width
    def in_segs(k_mode):
        return [(0, pool_width, None, FLAT), (o1, attn_width, 0, FLAT), (o2, attn_width, 1, k_mode),
                (o3, attn_width, None, HEADS), (o4, mem_width, 2, FLAT)]

    in_gains = [q_gain[l], k_gain[l], mq_gain[l]]

    def mix_and_ffn(x2d, pool_y, attn_y, mem_y, conv_prev, nseq, tt, scan=None):
        x1, h = _out_proj(x2d, pool_y, attn_y, mem_y, w_out_b, norm_ffn[l])
        act, sg, sv, *extra = _ffn_up(h, nseq, tt, w_up_b, conv_w[l], conv_b[l], conv_prev, scan=scan)
        y = _ffn_down(act, x1, w_down_b)
        return (y, jnp.concatenate([sg, sv], axis=-1), *extra)

    xp2 = x_prompt.reshape(bsz * seq, d)
    u, q, k, k_b, k_mean, v, v_b, mq = _norm_proj(xp2, norm_mix[l], w_in_b, in_gains, in_segs(HEADS_MEAN))
    u3 = u.reshape(bsz, seq, pool_width)
    pool_y = _pool_mix(u3, jnp.zeros((bsz, POOL_PAD, pool_width), F32), 0, pool_w_b, pool_scale[l])
    attn_y = _moba_prompt(q.reshape(bsz, seq, attn_width), k_b.reshape(bsz, seq, attn_width),
                          v_b.reshape(bsz, seq, attn_width), k_mean.reshape(bsz, seq // MOBA_BLOCK, attn_width),
                          slopes_tile, n_heads)
    mk, mv = _norm_proj(mem_prompt.reshape(bsz * n_mem, d), norm_mem[l], w_mem_b, [mk_gain[l]],
                        [(0, mem_width, 0, FLAT), (mem_width, mem_width, None, FLAT)])
    mem_y = _mem_attend(mq.reshape(bsz, seq, mem_width), mk.reshape(bsz, n_mem, mem_width),
                        mv.reshape(bsz, n_mem, mem_width))
    cache_k_flat = cache_k[l].reshape(n_pool * PAGE_SIZE * n_heads, HEAD_DIM)
    y_p, conv_st_p, kmean = mix_and_ffn(
        xp2, pool_y.reshape(bsz * seq, pool_width), attn_y.reshape(bsz * seq, attn_width),
        mem_y.reshape(bsz * seq, mem_width), jnp.zeros((bsz, CONV_CTX, f2), F32), 1, _row_tile(seq, 1024),
        scan=(cache_k_flat, page_table.reshape(-1), n_heads))
    pool_st_p = _last_rows(jnp.zeros((bsz, POOL_CTX, pool_width), F32), u3, POOL_CTX)

    xs2 = x_sample.reshape(db * tn, d)
    us, qs, ks, ks_b, vs, vs_b, mqs = _norm_proj(xs2, norm_mix[l], w_in_b, in_gains, in_segs(HEADS))
    us3 = us.reshape(db, tn, pool_width)
    prev_pool = jnp.concatenate([jnp.zeros((db, POOL_PAD - POOL_CTX, pool_width), F32), state_pool[l]], axis=1)
    pool_ys = _pool_mix(us3, prev_pool, past, pool_w_b, pool_scale[l])
    qs3, ks3, vs3 = (a.reshape(db, tn, attn_width) for a in (qs, ks_b, vs_b))
    n_sel = min(MOBA_TOPK, n_full)
    assert n_sel > 0
    idx_lanes = _gate_topk(qs3, kmean.reshape(db, n_full * n_heads, HEAD_DIM), n_heads, n_sel)
    idx = idx_lanes[:, :, :n_heads * TOPK_LANES].reshape(db, tn, n_heads, TOPK_LANES)[..., :n_sel]
    attn_ys = _moba_sample(qs3, ks3, vs3, cache_k[l], cache_v[l], page_table, idx.transpose(0, 2, 1, 3),
                           slopes_tile, n_heads)
    mem_ys = _mem_attend(mqs.reshape(db, tn, mem_width), cache_mem_k[l].reshape(db, n_mem, mem_width),
                         cache_mem_v[l].reshape(db, n_mem, mem_width))
    y_s, conv_st_s = mix_and_ffn(xs2, pool_ys.reshape(db * tn, pool_width), attn_ys.reshape(db * tn, attn_width),
                                 mem_ys.reshape(db * tn, mem_width), state_conv[l], db, tn)
    pool_st_s = _last_rows(state_pool[l], us3, POOL_CTX)

    hd = HEAD_DIM
    return (y_p.reshape(bsz, seq, d), y_s.reshape(db, tn, d),
            k.reshape(1, bsz, seq, n_heads, hd), v.reshape(1, bsz, seq, n_heads, hd),
            mk.reshape(1, bsz, n_mem, MEM_HEADS, hd), mv.reshape(1, bsz, n_mem, MEM_HEADS, hd),
            pool_st_p[None], conv_st_p[None],
            ks.reshape(1, db, tn, n_heads, hd), vs.reshape(1, db, tn, n_heads, hd),
            pool_st_s[None], conv_st_s[None])
```

```python
import functools

import jax
import jax.numpy as jnp
from jax import lax
from jax.experimental import pallas as pl
from jax.experimental.pallas import tpu as pltpu

HEAD_DIM = 128
POOL_WINDOWS = (2, 4, 8, 16)
POOL_CTX = max(POOL_WINDOWS) - 1
POOL_PAD = 16
MEM_HEADS = 4
MOBA_BLOCK = 256
MOBA_TOPK = 3
PAGE_SIZE = 128
CONV_WIDTH = 3
CONV_CTX = CONV_WIDTH - 1
CONV_PAD = 8
FFN_COLS = 512
NORM_EPS = 1e-6
NEG_INF = -1e30

VMEM_LIMIT_BYTES = 56 * 1024 * 1024

BF16 = jnp.bfloat16
F32 = jnp.float32


def _params(*semantics):
    return pltpu.CompilerParams(dimension_semantics=semantics, vmem_limit_bytes=VMEM_LIMIT_BYTES)


def _whole_vmem():
    return pl.BlockSpec(memory_space=pltpu.VMEM)


def _row_tile(n, want):
    if n <= want:
        return n
    for t in range(want - want % 8, 0, -8):
        if n % t == 0:
            return t
    raise ValueError((n, want))


def _dot(a, b):
    return jnp.dot(a, b, preferred_element_type=F32)


def _dot_nt(a, b, precision=None):
    return lax.dot_general(a, b, (((1,), (1,)), ((), ())), precision=precision, preferred_element_type=F32)


def _rms_scale(x):
    return lax.rsqrt(jnp.mean(x * x, axis=-1, keepdims=True) + NORM_EPS)


FLAT, HEADS, HEADS_MEAN = "flat", "heads", "heads_mean"
_N_OUTS = {FLAT: 1, HEADS: 2, HEADS_MEAN: 3}


def _norm_proj_kernel(x_ref, nw_ref, w_ref, *rest, segs, n_gain, chunk):
    gain_refs = rest[:n_gain]
    out_refs = list(rest[n_gain:])
    tm = x_ref.shape[0]
    x = x_ref[...]
    xn = (x * _rms_scale(x) * nw_ref[...]).astype(BF16)
    for start, width, gi, mode in segs:
        outs = [out_refs.pop(0) for _ in range(_N_OUTS[mode])]
        heads = width // HEAD_DIM
        for c0 in range(0, width, chunk):
            z = _dot(xn, w_ref[:, start + c0:start + c0 + chunk])
            for h0 in range(0, chunk, HEAD_DIM):
                zh = z[:, h0:h0 + HEAD_DIM]
                if gi is not None:
                    zh = zh * _rms_scale(zh) * gain_refs[gi][...]
                col = slice(c0 + h0, c0 + h0 + HEAD_DIM)
                if mode == FLAT:
                    outs[0][:, col] = zh
                    continue
                outs[0][pl.ds((c0 + h0) // HEAD_DIM, tm, stride=heads), :] = zh
                outs[1][:, col] = zh.astype(BF16)
                if mode == HEADS_MEAN:
                    for r in range(tm // MOBA_BLOCK):
                        outs[2][0, r:r + 1, col] = jnp.mean(zh[r * MOBA_BLOCK:(r + 1) * MOBA_BLOCK], axis=0,
                                                            keepdims=True)


def _norm_proj(x2d, norm_w, w_bf16, gains, segs, tm_want=512):
    n, d = x2d.shape
    tm = _row_tile(n, tm_want)
    chunk = 512
    assert all(w % chunk == 0 for _, w, _, _ in segs)
    in_specs = [pl.BlockSpec((tm, d), lambda i: (i, 0)), _whole_vmem(), _whole_vmem()]
    in_specs += [_whole_vmem() for _ in gains]
    out_specs, out_shape = [], []
    for _, w, _, mode in segs:
        heads = w // HEAD_DIM
        if mode == FLAT:
            out_specs.append(pl.BlockSpec((tm, w), lambda i: (i, 0)))
            out_shape.append(jax.ShapeDtypeStruct((n, w), F32))
            continue
        out_specs.append(pl.BlockSpec((tm * heads, HEAD_DIM), lambda i: (i, 0)))
        out_shape.append(jax.ShapeDtypeStruct((n * heads, HEAD_DIM), F32))
        out_specs.append(pl.BlockSpec((tm, w), lambda i: (i, 0)))
        out_shape.append(jax.ShapeDtypeStruct((n, w), BF16))
        if mode == HEADS_MEAN:
            assert tm % MOBA_BLOCK == 0
            out_specs.append(pl.BlockSpec((1, tm // MOBA_BLOCK, w), lambda i: (i, 0, 0)))
            out_shape.append(jax.ShapeDtypeStruct((n // tm, tm // MOBA_BLOCK, w), F32))
    kern = functools.partial(_norm_proj_kernel, segs=tuple(segs), n_gain=len(gains), chunk=chunk)
    return pl.pallas_call(
        kern, out_shape=out_shape, grid=(n // tm,), in_specs=in_specs, out_specs=out_specs,
        compiler_params=_params("arbitrary"), name="norm_proj",
    )(x2d, norm_w.reshape(1, d), w_bf16, *[g.reshape(1, HEAD_DIM) for g in gains])


def _pool_kernel(u_ref, prev_ref, pw_ref, ps_ref, y_ref, ext_ref, *, tt, pos0):
    j = pl.program_id(1)

    @pl.when(j == 0)
    def _():
        ext_ref[0:POOL_PAD, :] = prev_ref[0]

    @pl.when(j > 0)
    def _():
        ext_ref[0:POOL_PAD, :] = ext_ref[tt:tt + POOL_PAD, :]

    u = u_ref[0]
    ext_ref[POOL_PAD:POOL_PAD + tt, :] = u
    pos = pos0 + j * tt + lax.broadcasted_iota(jnp.int32, (tt, 1), 0)
    for g, w in enumerate(POOL_WINDOWS):
        c = slice(g * HEAD_DIM, (g + 1) * HEAD_DIM)
        ug = u[:, c]
        acc = ug
        for s in range(1, w):
            acc = acc + ext_ref[POOL_PAD - s:POOL_PAD - s + tt, c]
        cnt = jnp.minimum(pos + 1, w).astype(F32)
        diff = acc / cnt - ug
        y_ref[0, :, c] = _dot(diff.astype(BF16), pw_ref[g]) * ps_ref[:, c]


def _pool_mix(u, prev, pos0, pool_w_bf16, pool_scale, tt_want=1024):
    b, t, c = u.shape
    assert c == HEAD_DIM * len(POOL_WINDOWS)
    tt = _row_tile(t, tt_want)
    assert tt == t or tt >= POOL_PAD
    kern = functools.partial(_pool_kernel, tt=tt, pos0=pos0)
    return pl.pallas_call(
        kern, out_shape=jax.ShapeDtypeStruct((b, t, c), F32), grid=(b, t // tt),
        in_specs=[pl.BlockSpec((1, tt, c), lambda i, j: (i, j, 0)),
                  pl.BlockSpec((1, POOL_PAD, c), lambda i, j: (i, 0, 0)),
                  _whole_vmem(), _whole_vmem()],
        out_specs=pl.BlockSpec((1, tt, c), lambda i, j: (i, j, 0)),
        scratch_shapes=[pltpu.VMEM((POOL_PAD + tt, c), F32)],
        compiler_params=_params("arbitrary", "arbitrary"), name="pool_mix",
    )(u, prev, pool_w_bf16, pool_scale.reshape(1, c))


def _moba_prompt_kernel(slope_ref, q_ref, k_ref, v_ref, km_ref, o_ref, *, t_len):
    L = MOBA_BLOCK
    nblk = t_len // L
    scale = HEAD_DIM ** -0.5
    slope = slope_ref[0, 0:1, 0:1]
    q = q_ref[0]
    kb = k_ref[0]
    vb = v_ref[0]
    n_sel = min(MOBA_TOPK, nblk - 1)
    assert nblk <= 8 and nblk + 3 <= HEAD_DIM
    F_LO, F_HI, F_ONE = nblk, nblk + 1, nblk + 2
    row = lax.broadcasted_iota(jnp.int32, (t_len, HEAD_DIM), 0)
    lane = lax.broadcasted_iota(jnp.int32, (t_len, HEAD_DIM), 1)
    blk = row // L
    blk_f = blk.astype(F32)
    k_feat = jnp.where(lane == blk, 1.0, 0.0)
    k_feat = jnp.where(lane == F_LO, (row - blk * L).astype(F32), k_feat)
    k_feat = jnp.where(lane == F_HI, blk_f, k_feat)
    k_feat = jnp.where(lane == F_ONE, 1.0, k_feat)
    k_aug = jnp.concatenate([kb, k_feat.astype(BF16)], axis=-1)

    q_feat = jnp.zeros((t_len, HEAD_DIM), F32)
    if n_sel > 0:
        km = jnp.concatenate([km_ref[0], jnp.zeros((HEAD_DIM - nblk, HEAD_DIM), F32)], axis=0)
        km_hi = km.astype(BF16)
        km_lo = (km - km_hi.astype(F32)).astype(BF16)
        r0 = L
        qr = q[r0:]
        q_hi = qr.astype(BF16)
        q_lo = (qr - q_hi.astype(F32)).astype(BF16)
        gate = (_dot_nt(jnp.concatenate([q_hi, q_lo], axis=-1), jnp.concatenate([km_hi, km_hi], axis=-1))
                + _dot_nt(q_hi, km_lo))
        g = gate.T[0:8]
        sub = lax.broadcasted_iota(jnp.int32, g.shape, 0)
        n_full = (r0 + lax.broadcasted_iota(jnp.int32, g.shape, 1)) // L
        gm = jnp.where(sub < n_full, g, NEG_INF)
        pen = jnp.zeros(g.shape, F32)
        for j in range(nblk - 1):
            gj = gm[j:j + 1, :]
            beats = (gm > gj) | ((gm == gj) & (sub < j))
            rank = jnp.sum(beats.astype(F32), axis=0, keepdims=True)
            sel = (rank < n_sel) & (n_full > j)
            pen = jnp.where(sub == j, jnp.where(sel, 0.0, NEG_INF), pen)
        pen = jnp.where(sub < n_full, pen, 0.0)
        pen_t = jnp.concatenate([pen, jnp.zeros((HEAD_DIM - 8, pen.shape[1]), F32)], axis=0).T
        q_feat = jnp.concatenate([jnp.zeros((r0, HEAD_DIM), F32), pen_t], axis=0)
    q_feat = jnp.where(lane == F_LO, slope, q_feat)
    q_feat = jnp.where(lane == F_HI, slope * float(L), q_feat)
    q_feat = jnp.where(lane == F_ONE, -slope * float(L) * blk_f, q_feat)
    q_aug = jnp.concatenate([(q * scale).astype(BF16), q_feat.astype(BF16)], axis=-1)

    causal = (lax.broadcasted_iota(jnp.int32, (L, L), 0) >= lax.broadcasted_iota(jnp.int32, (L, L), 1))
    for n in range(nblk):
        s = _dot_nt(q_aug[n * L:(n + 1) * L], k_aug[0:(n + 1) * L])
        s_own = jnp.where(causal, s[:, n * L:], NEG_INF)
        m = jnp.max(s_own, axis=-1, keepdims=True)
        if n > 0:
            s_past = s[:, :n * L]
            m = jnp.maximum(m, jnp.max(s_past, axis=-1, keepdims=True))
        p_own = jnp.exp(s_own - m)
        l = jnp.sum(p_own, axis=-1, keepdims=True)
        acc = _dot(p_own.astype(BF16), vb[n * L:(n + 1) * L])
        if n > 0:
            p_past = jnp.exp(s_past - m)
            l = l + jnp.sum(p_past, axis=-1, keepdims=True)
            acc = acc + _dot(p_past.astype(BF16), vb[:n * L])
        o_ref[0, n * L:(n + 1) * L, :] = acc / l


def _moba_prompt(q, k_bf16, v_bf16, kmean, slopes_tile, n_heads):
    b, t, w = q.shape
    assert t % MOBA_BLOCK == 0 and w == n_heads * HEAD_DIM
    nblk = t // MOBA_BLOCK
    spec = pl.BlockSpec((1, t, HEAD_DIM), lambda i, h: (i, 0, h))
    kern = functools.partial(_moba_prompt_kernel, t_len=t)
    return pl.pallas_call(
        kern, out_shape=jax.ShapeDtypeStruct((b, t, w), F32), grid=(b, n_heads),
        in_specs=[pl.BlockSpec((1, 8, HEAD_DIM), lambda i, h: (h, 0, 0)), spec, spec, spec,
                  pl.BlockSpec((1, nblk, HEAD_DIM), lambda i, h: (i, 0, h))],
        out_specs=spec,
        compiler_params=_params("arbitrary", "arbitrary"), name="moba_prompt",
    )(slopes_tile, q, k_bf16, v_bf16, kmean)


def _mem_attn_kernel(q_ref, k_ref, v_ref, o_ref, *, n_heads, head_rows):
    scale = HEAD_DIM ** -0.5
    n_mem = k_ref.shape[1] // n_heads if head_rows else k_ref.shape[1]
    for h in range(n_heads):
        c = slice(h * HEAD_DIM, (h + 1) * HEAD_DIM)
        if head_rows:
            kh = k_ref[0, pl.ds(h, n_mem, stride=n_heads), :].astype(BF16)
            vh = v_ref[0, pl.ds(h, n_mem, stride=n_heads), :].astype(BF16)
        else:
            kh = k_ref[0, :, c].astype(BF16)
            vh = v_ref[0, :, c].astype(BF16)
        s = _dot_nt(q_ref[0, :, c].astype(BF16), kh) * scale
        m = jnp.max(s, axis=-1, keepdims=True)
        p = jnp.exp(s - m)
        l = jnp.sum(p, axis=-1, keepdims=True)
        o_ref[0, :, c] = _dot(p.astype(BF16), vh) / l


def _mem_attend(mq, mk, mv, n_heads, head_rows, tq_want=1024):
    b, t, w = mq.shape
    tq = _row_tile(t, tq_want)
    qspec = pl.BlockSpec((1, tq, w), lambda i, j: (i, j, 0))
    kspec = pl.BlockSpec((1,) + mk.shape[1:], lambda i, j: (i, 0, 0))
    return pl.pallas_call(
        functools.partial(_mem_attn_kernel, n_heads=n_heads, head_rows=head_rows),
        out_shape=jax.ShapeDtypeStruct((b, t, w), F32),
        grid=(b, t // tq), in_specs=[qspec, kspec, kspec], out_specs=qspec,
        compiler_params=_params("arbitrary", "arbitrary"), name="mem_attend",
    )(mq, mk, mv)


def _out_proj_kernel(x_ref, p_ref, a_ref, m_ref, w_ref, nw_ref, x1_ref, h_ref, mix_ref):
    o1 = p_ref.shape[1]
    o2 = o1 + a_ref.shape[1]
    mix_ref[:, 0:o1] = p_ref[...].astype(BF16)
    mix_ref[:, o1:o2] = a_ref[...].astype(BF16)
    mix_ref[:, o2:] = m_ref[...].astype(BF16)
    x1 = x_ref[...] + _dot(mix_ref[...], w_ref[...])
    x1_ref[...] = x1
    h_ref[...] = (x1 * _rms_scale(x1) * nw_ref[...]).astype(BF16)


def _out_proj(x2d, pool_y, attn_y, mem_y, w_out_bf16, norm_ffn, tm_want=512):
    n, d = x2d.shape
    tm = _row_tile(n, tm_want)
    widths = (pool_y.shape[1], attn_y.shape[1], mem_y.shape[1])
    assert sum(widths) == w_out_bf16.shape[0]
    row = lambda w: pl.BlockSpec((tm, w), lambda i: (i, 0))
    return pl.pallas_call(
        _out_proj_kernel,
        out_shape=[jax.ShapeDtypeStruct((n, d), F32), jax.ShapeDtypeStruct((n, d), BF16)],
        grid=(n // tm,),
        in_specs=[row(d), row(widths[0]), row(widths[1]), row(widths[2]), _whole_vmem(), _whole_vmem()],
        out_specs=[row(d), row(d)],
        scratch_shapes=[pltpu.VMEM((tm, sum(widths)), BF16)],
        compiler_params=_params("arbitrary"), name="out_proj",
    )(x2d, pool_y, attn_y, mem_y, w_out_bf16, norm_ffn.reshape(1, d))


def _ffn_up_kernel(h_ref, wg_ref, wv_ref, cwg_ref, cwv_ref, cbg_ref, cbv_ref, pg_ref, pv_ref,
                   a_ref, sg_ref, sv_ref, eg_ref, ev_ref, *, nseq, tt, fc, sub, riders=()):
    j = pl.program_id(2)
    riders = list(riders)
    lo = CONV_PAD - CONV_CTX

    @pl.when(j == 0)
    def _():
        eg_ref[:, lo:CONV_PAD, :] = pg_ref[...]
        ev_ref[:, lo:CONV_PAD, :] = pv_ref[...]

    @pl.when(j > 0)
    def _():
        eg_ref[:, lo:CONV_PAD, :] = eg_ref[:, tt + lo:tt + CONV_PAD, :]
        ev_ref[:, lo:CONV_PAD, :] = ev_ref[:, tt + lo:tt + CONV_PAD, :]

    nc = min(fc, FFN_COLS)

    def conv(hm, r0, cols, w_ref, cw_ref, cb_ref, ext_ref):
        u = _dot(hm, w_ref[:, cols]).reshape(nseq, sub, nc)
        ext_ref[:, pl.ds(CONV_PAD + r0, sub), cols] = u
        win = ext_ref[:, pl.ds(r0, sub + CONV_PAD), cols]
        out = cb_ref[:, cols].reshape(1, 1, nc)
        for kk in range(CONV_WIDTH):
            out = out + win[:, lo + kk:lo + kk + sub, :] * cw_ref[kk:kk + 1, cols].reshape(1, 1, nc)
        return out

    def piece(r0):
        rows = pl.ds(r0 * nseq, sub * nseq)
        hm = h_ref[rows, :]
        for c0 in range(0, fc, nc):
            cols = slice(c0, c0 + nc)
            g = conv(hm, r0, cols, wg_ref, cwg_ref, cbg_ref, eg_ref)
            val = conv(hm, r0, cols, wv_ref, cwv_ref, cbv_ref, ev_ref)
            act = g * (1.0 / (1.0 + jnp.exp(-g))) * val
            a_ref[rows, cols] = act.reshape(nseq * sub, nc).astype(BF16)

    for r0 in range(0, tt, sub):
        piece(r0)
        if riders:
            riders.pop(0)()
    sg_ref[...] = eg_ref[:, CONV_PAD + tt - CONV_CTX:CONV_PAD + tt, :]
    sv_ref[...] = ev_ref[:, CONV_PAD + tt - CONV_CTX:CONV_PAD + tt, :]
    for rider in riders:
        rider()


def _ffn_up_scan_kernel(pt_ref, *refs, n_scan, n_heads, **kw):
    del pt_ref
    n_ffn_in = 9
    page_refs = refs[n_ffn_in:n_ffn_in + n_scan]
    km_ref = refs[n_ffn_in + n_scan + 3]
    ppb = MOBA_BLOCK // PAGE_SIZE

    def block_mean(n):
        s = page_refs[n * ppb][...]
        for p in range(1, ppb):
            s = s + page_refs[n * ppb + p][...]
        km_ref[n * n_heads:(n + 1) * n_heads, :] = (
            jnp.sum(s.reshape(PAGE_SIZE, n_heads, HEAD_DIM), axis=0) * (1.0 / MOBA_BLOCK))

    _ffn_up_kernel(*refs[:n_ffn_in], *refs[n_ffn_in + n_scan:n_ffn_in + n_scan + 3], *refs[n_ffn_in + n_scan + 4:],
                   riders=[functools.partial(block_mean, n) for n in range(n_scan // ppb)], **kw)


def _ffn_up(h2d, nseq, tt, w_up_bf16, conv_w, conv_b, prev, fc=512, scan=None):
    n, d = h2d.shape
    f2 = w_up_bf16.shape[1]
    f = f2 // 2
    assert f % fc == 0
    nfc = f // fc
    n_seqs = prev.shape[0]
    seq_len = n // n_seqs
    assert seq_len % tt == 0 and (nseq == 1 or tt == seq_len) and n_seqs % nseq == 0
    tiles_per_seq = seq_len // tt
    rows = nseq * tt
    sub = _row_tile(tt, 128) if nseq == 1 else tt
    grid = (n_seqs // nseq, nfc, tiles_per_seq)
    kw = dict(nseq=nseq, tt=tt, fc=fc, sub=sub)
    wspec = lambda off: pl.BlockSpec((d, fc), lambda i, c, j, *_: (0, c + off))
    vspec = lambda r, off: pl.BlockSpec((r, fc), lambda i, c, j, *_: (0, c + off))
    pspec = lambda off: pl.BlockSpec((nseq, CONV_CTX, fc), lambda i, c, j, *_: (i, 0, c + off))
    sspec = pl.BlockSpec((nseq, CONV_CTX, fc), lambda i, c, j, *_: (i, 0, c))
    in_specs = [pl.BlockSpec((rows, d), lambda i, c, j, *_: (i * tiles_per_seq + j, 0)),
                wspec(0), wspec(nfc), vspec(CONV_WIDTH, 0), vspec(CONV_WIDTH, nfc),
                vspec(1, 0), vspec(1, nfc), pspec(0), pspec(nfc)]
    out_specs = [pl.BlockSpec((rows, fc), lambda i, c, j, *_: (i * tiles_per_seq + j, c)), sspec, sspec]
    out_shape = [jax.ShapeDtypeStruct((n, f), BF16),
                 jax.ShapeDtypeStruct((n_seqs, CONV_CTX, f), F32),
                 jax.ShapeDtypeStruct((n_seqs, CONV_CTX, f), F32)]
    scratch = [pltpu.VMEM((nseq, CONV_PAD + tt, fc), F32), pltpu.VMEM((nseq, CONV_PAD + tt, fc), F32)]
    args = (h2d, w_up_bf16, w_up_bf16, conv_w, conv_w, conv_b.reshape(1, f2), conv_b.reshape(1, f2), prev, prev)
    if scan is None:
        return pl.pallas_call(
            functools.partial(_ffn_up_kernel, **kw), out_shape=out_shape, grid=grid, in_specs=in_specs,
            out_specs=out_specs, scratch_shapes=scratch,
            compiler_params=_params("arbitrary", "arbitrary", "arbitrary"), name="ffn_up")(*args)

    cache_flat, pages, n_heads = scan
    ppb = MOBA_BLOCK // PAGE_SIZE
    n_steps = grid[0] * grid[1] * grid[2]
    total = pages.shape[0]
    assert total % ppb == 0
    n_scan = -(-total // (n_steps * ppb)) * ppb
    page_rows = PAGE_SIZE * n_heads
    km_rows = n_scan // ppb * n_heads

    def step_of(i, c, j):
        return (i * nfc + c) * tiles_per_seq + j

    def page_spec(p):
        return pl.BlockSpec((page_rows, HEAD_DIM),
                            lambda i, c, j, pt: (pt[jnp.minimum(step_of(i, c, j) * n_scan + p, total - 1)], 0))

    grid_spec = pltpu.PrefetchScalarGridSpec(
        num_scalar_prefetch=1, grid=grid,
        in_specs=in_specs + [page_spec(p) for p in range(n_scan)],
        out_specs=out_specs + [pl.BlockSpec((km_rows, HEAD_DIM), lambda i, c, j, pt: (step_of(i, c, j), 0))],
        scratch_shapes=scratch)
    act, sg, sv, km = pl.pallas_call(
        functools.partial(_ffn_up_scan_kernel, n_scan=n_scan, n_heads=n_heads, **kw),
        out_shape=out_shape + [jax.ShapeDtypeStruct((n_steps * km_rows, HEAD_DIM), F32)], grid_spec=grid_spec,
        compiler_params=_params("arbitrary", "arbitrary", "arbitrary"), name="ffn_up_scan",
    )(pages, *args, *([cache_flat] * n_scan))
    return act, sg, sv, km[:total // ppb * n_heads]


def _ffn_down_kernel(a_ref, x_ref, w_ref, y_ref):
    y_ref[...] = x_ref[...] + _dot(a_ref[...], w_ref[...])


def _ffn_down(act, x1, w_down_bf16, tm_want=256):
    n, f = act.shape
    d = x1.shape[1]
    tm = _row_tile(n, tm_want)
    return pl.pallas_call(
        _ffn_down_kernel, out_shape=jax.ShapeDtypeStruct((n, d), F32), grid=(n // tm,),
        in_specs=[pl.BlockSpec((tm, f), lambda i: (i, 0)), pl.BlockSpec((tm, d), lambda i: (i, 0)), _whole_vmem()],
        out_specs=pl.BlockSpec((tm, d), lambda i: (i, 0)),
        compiler_params=_params("arbitrary"), name="ffn_down",
    )(act, x1, w_down_bf16)


TOPK_LANES = 4


def _gate_topk_kernel(q_ref, km_ref, idx_ref, *, n_heads, n_sel):
    q = q_ref[0]
    tn = q.shape[0]
    n_full = km_ref.shape[1] // n_heads
    lane = lax.broadcasted_iota(jnp.int32, (tn, n_full), 1).astype(F32)
    out_lane = lax.broadcasted_iota(jnp.int32, (tn, HEAD_DIM), 1)
    out = jnp.zeros((tn, HEAD_DIM), F32)
    for h in range(n_heads):
        c = slice(h * HEAD_DIM, (h + 1) * HEAD_DIM)
        km = km_ref[0, pl.ds(h, n_full, stride=n_heads), :]
        g = _dot_nt(q[:, c], km, precision=lax.Precision.HIGHEST)
        for r in range(n_sel):
            m = jnp.max(g, axis=-1, keepdims=True)
            idx = jnp.min(jnp.where(g == m, lane, float(n_full)), axis=-1, keepdims=True)
            out = jnp.where(out_lane == h * TOPK_LANES + r, idx, out)
            g = jnp.where(lane == idx, -jnp.inf, g)
    idx_ref[0] = out.astype(jnp.int32)


def _gate_topk(q_s, kmean, n_heads, n_sel):
    db, tn, w = q_s.shape
    rows = kmean.shape[1]
    assert n_heads * TOPK_LANES <= HEAD_DIM and n_sel <= TOPK_LANES
    kern = functools.partial(_gate_topk_kernel, n_heads=n_heads, n_sel=n_sel)
    return pl.pallas_call(
        kern, out_shape=jax.ShapeDtypeStruct((db, tn, HEAD_DIM), jnp.int32), grid=(db,),
        in_specs=[pl.BlockSpec((1, tn, w), lambda b: (b, 0, 0)),
                  pl.BlockSpec((1, rows, HEAD_DIM), lambda b: (b, 0, 0))],
        out_specs=pl.BlockSpec((1, tn, HEAD_DIM), lambda b: (b, 0, 0)),
        compiler_params=_params("arbitrary"), name="gate_topk",
    )(q_s, kmean)


def _moba_sample_kernel(pt_ref, idx_ref, slope_ref, q_ref, kn_ref, vn_ref, ck_ref, cv_ref, o_ref,
                        kbuf, vbuf, sem, *, tn, n_sel, n_heads, past):
    L = MOBA_BLOCK
    ppb = L // PAGE_SIZE
    n_slots = tn * n_sel * ppb
    b = pl.program_id(0)
    h = pl.program_id(1)
    step = b * n_heads + h
    n_steps = pl.num_programs(0) * n_heads
    cur = lax.rem(step, 2)
    scale = HEAD_DIM ** -0.5

    def block_of(bb, hh, t, r):
        return idx_ref[((bb * n_heads + hh) * tn + t) * n_sel + r]

    def gathers(bb, hh, buf_slot):
        out = []
        for t in range(tn):
            for r in range(n_sel):
                blk = block_of(bb, hh, t, r)
                for p in range(ppb):
                    page = pt_ref[bb, blk * ppb + p]
                    i = (t * n_sel + r) * ppb + p
                    out.append(pltpu.make_async_copy(ck_ref.at[page, :, hh, :], kbuf.at[buf_slot, i],
                                                     sem.at[0, buf_slot]))
                    out.append(pltpu.make_async_copy(cv_ref.at[page, :, hh, :], vbuf.at[buf_slot, i],
                                                     sem.at[1, buf_slot]))
        return out

    @pl.when(step == 0)
    def _():
        for c in gathers(b, h, cur):
            c.start()

    @pl.when(step + 1 < n_steps)
    def _():
        nxt = step + 1
        for c in gathers(nxt // n_heads, lax.rem(nxt, n_heads), 1 - cur):
            c.start()

    for c in gathers(b, h, cur):
        c.wait()

    slope = slope_ref[0, 0:1, 0:1]
    qb = q_ref[0].astype(BF16)
    n_keys = n_slots * PAGE_SIZE
    s = _dot_nt(qb, kbuf[cur].reshape(n_keys, HEAD_DIM).astype(BF16)) * scale
    lane = lax.broadcasted_iota(jnp.int32, (tn, PAGE_SIZE), 1)
    k_pos = jnp.concatenate(
        [block_of(b, h, t, r) * L + p * PAGE_SIZE + lane for t in range(tn) for r in range(n_sel) for p in range(ppb)],
        axis=1)
    row = lax.broadcasted_iota(jnp.int32, (tn, n_keys), 0)
    col = lax.broadcasted_iota(jnp.int32, (tn, n_keys), 1)
    per_query = n_sel * L
    mine = (col >= row * per_query) & (col < row * per_query + per_query)
    s = jnp.where(mine, s - slope * (past + row - k_pos).astype(F32), NEG_INF)
    row_n = lax.broadcasted_iota(jnp.int32, (tn, tn), 0)
    lane_n = lax.broadcasted_iota(jnp.int32, (tn, tn), 1)
    s_own = _dot_nt(qb, kn_ref[0]) * scale - slope * (row_n - lane_n).astype(F32)
    s_own = jnp.where(lane_n <= row_n, s_own, NEG_INF)
    m = jnp.maximum(jnp.max(s, axis=-1, keepdims=True), jnp.max(s_own, axis=-1, keepdims=True))
    p = jnp.exp(s - m)
    p_own = jnp.exp(s_own - m)
    l = jnp.sum(p, axis=-1, keepdims=True) + jnp.sum(p_own, axis=-1, keepdims=True)
    acc = _dot(p.astype(BF16), vbuf[cur].reshape(n_keys, HEAD_DIM).astype(BF16)) + _dot(p_own.astype(BF16), vn_ref[0])
    o_ref[0] = acc / l


def _moba_sample(q_s, k_new_bf16, v_new_bf16, cache_k4, cache_v4, page_table, idx, slopes_tile, n_heads):
    db, tn, w = q_s.shape
    n_sel = idx.shape[-1]
    ppb = MOBA_BLOCK // PAGE_SIZE
    past = page_table.shape[1] * PAGE_SIZE
    n_slots = tn * n_sel * ppb
    new_spec = pl.BlockSpec((1, tn, HEAD_DIM), lambda b, h, pt, ix: (b, 0, h))
    hbm = pl.BlockSpec(memory_space=pl.ANY)
    grid_spec = pltpu.PrefetchScalarGridSpec(
        num_scalar_prefetch=2, grid=(db, n_heads),
        in_specs=[pl.BlockSpec((1, 8, HEAD_DIM), lambda b, h, pt, ix: (h, 0, 0)), new_spec, new_spec, new_spec,
                  hbm, hbm],
        out_specs=new_spec,
        scratch_shapes=[pltpu.VMEM((2, n_slots, PAGE_SIZE, HEAD_DIM), F32),
                        pltpu.VMEM((2, n_slots, PAGE_SIZE, HEAD_DIM), F32),
                        pltpu.SemaphoreType.DMA((2, 2))])
    kern = functools.partial(_moba_sample_kernel, tn=tn, n_sel=n_sel, n_heads=n_heads, past=past)
    return pl.pallas_call(
        kern, out_shape=jax.ShapeDtypeStruct((db, tn, w), F32), grid_spec=grid_spec,
        compiler_params=_params("arbitrary", "arbitrary"), name="moba_sample",
    )(page_table, idx.reshape(-1), slopes_tile, q_s, k_new_bf16, v_new_bf16, cache_k4, cache_v4)


def _last_rows(prev, new, n):
    t = new.shape[1]
    if t >= n:
        return new[:, t - n:]
    return jnp.concatenate([prev[:, prev.shape[1] - (n - t):], new], axis=1)


def kernel(x_prompt, x_sample, cache_k, cache_v, cache_mem_k, cache_mem_v, state_pool, state_conv, page_table, mem_prompt, norm_mix, w_in, pool_w, pool_scale, q_gain, k_gain, norm_mem, w_mem_kv, mq_gain, mk_gain, w_out, norm_ffn, w_up, conv_w, conv_b, w_down):
    depth = w_in.shape[0]
    assert depth == 1
    bsz, seq, d = x_prompt.shape
    db, tn, _ = x_sample.shape
    pool_width = pool_w.shape[1] * pool_w.shape[2]
    n_heads = cache_k.shape[3]
    attn_width = n_heads * HEAD_DIM
    mem_width = MEM_HEADS * HEAD_DIM
    n_mem = mem_prompt.shape[1]
    n_pool = cache_k.shape[1]
    n_pages = page_table.shape[1]
    past = n_pages * PAGE_SIZE
    n_full = past // MOBA_BLOCK
    assert past % MOBA_BLOCK == 0
    f2 = w_up.shape[2]

    slopes = jnp.exp2(-8.0 * jnp.arange(1, n_heads + 1, dtype=F32) / n_heads)
    slopes_tile = jnp.broadcast_to(slopes[:, None, None], (n_heads, 8, HEAD_DIM))

    l = 0
    w_in_b = w_in[l].astype(BF16)
    w_out_b = w_out[l].astype(BF16)
    w_up_b = w_up[l].astype(BF16)
    w_down_b = w_down[l].astype(BF16)
    w_mem_b = w_mem_kv[l].astype(BF16)
    pool_w_b = pool_w[l].astype(BF16)

    o1 = pool_width
    o2 = o1 + attn_width
    o3 = o2 + attn_width
    o4 = o3 + attn_width
    def in_segs(k_mode):
        return [(0, pool_width, None, FLAT), (o1, attn_width, 0, FLAT), (o2, attn_width, 1, k_mode),
                (o3, attn_width, None, HEADS), (o4, mem_width, 2, FLAT)]

    in_gains = [q_gain[l], k_gain[l], mq_gain[l]]

    def mix_and_ffn(x2d, pool_y, attn_y, mem_y, conv_prev, nseq, tt, scan=None):
        x1, h = _out_proj(x2d, pool_y, attn_y, mem_y, w_out_b, norm_ffn[l])
        act, sg, sv, *extra = _ffn_up(h, nseq, tt, w_up_b, conv_w[l], conv_b[l], conv_prev, scan=scan)
        y = _ffn_down(act, x1, w_down_b)
        return (y, jnp.concatenate([sg, sv], axis=-1), *extra)

    xp2 = x_prompt.reshape(bsz * seq, d)
    u, q, k, k_b, k_mean, v, v_b, mq = _norm_proj(xp2, norm_mix[l], w_in_b, in_gains, in_segs(HEADS_MEAN))
    u3 = u.reshape(bsz, seq, pool_width)
    pool_y = _pool_mix(u3, jnp.zeros((bsz, POOL_PAD, pool_width), F32), 0, pool_w_b, pool_scale[l])
    attn_y = _moba_prompt(q.reshape(bsz, seq, attn_width), k_b.reshape(bsz, seq, attn_width),
                          v_b.reshape(bsz, seq, attn_width), k_mean.reshape(bsz, seq // MOBA_BLOCK, attn_width),
                          slopes_tile, n_heads)
    mk, mk_b, mv, mv_b = _norm_proj(mem_prompt.reshape(bsz * n_mem, d), norm_mem[l], w_mem_b, [mk_gain[l]],
                                    [(0, mem_width, 0, HEADS), (mem_width, mem_width, None, HEADS)])
    mem_y = _mem_attend(mq.reshape(bsz, seq, mem_width), mk_b.reshape(bsz, n_mem, mem_width),
                        mv_b.reshape(bsz, n_mem, mem_width), MEM_HEADS, head_rows=False, tq_want=2048)
    cache_k_flat = cache_k[l].reshape(n_pool * PAGE_SIZE * n_heads, HEAD_DIM)
    y_p, conv_st_p, kmean = mix_and_ffn(
        xp2, pool_y.reshape(bsz * seq, pool_width), attn_y.reshape(bsz * seq, attn_width),
        mem_y.reshape(bsz * seq, mem_width), jnp.zeros((bsz, CONV_CTX, f2), F32), 1, _row_tile(seq, 1024),
        scan=(cache_k_flat, page_table.reshape(-1), n_heads))
    pool_st_p = _last_rows(jnp.zeros((bsz, POOL_CTX, pool_width), F32), u3, POOL_CTX)

    xs2 = x_sample.reshape(db * tn, d)
    us, qs, ks, ks_b, vs, vs_b, mqs = _norm_proj(xs2, norm_mix[l], w_in_b, in_gains, in_segs(HEADS))
    us3 = us.reshape(db, tn, pool_width)
    prev_pool = jnp.concatenate([jnp.zeros((db, POOL_PAD - POOL_CTX, pool_width), F32), state_pool[l]], axis=1)
    pool_ys = _pool_mix(us3, prev_pool, past, pool_w_b, pool_scale[l])
    qs3, ks3, vs3 = (a.reshape(db, tn, attn_width) for a in (qs, ks_b, vs_b))
    n_sel = min(MOBA_TOPK, n_full)
    assert n_sel > 0
    idx_lanes = _gate_topk(qs3, kmean.reshape(db, n_full * n_heads, HEAD_DIM), n_heads, n_sel)
    idx = idx_lanes[:, :, :n_heads * TOPK_LANES].reshape(db, tn, n_heads, TOPK_LANES)[..., :n_sel]
    attn_ys = _moba_sample(qs3, ks3, vs3, cache_k[l], cache_v[l], page_table, idx.transpose(0, 2, 1, 3),
                           slopes_tile, n_heads)
    mem_ys = _mem_attend(mqs.reshape(db, tn, mem_width), cache_mem_k[l].reshape(db, n_mem * MEM_HEADS, HEAD_DIM),
                         cache_mem_v[l].reshape(db, n_mem * MEM_HEADS, HEAD_DIM), MEM_HEADS, head_rows=True)
    y_s, conv_st_s = mix_and_ffn(xs2, pool_ys.reshape(db * tn, pool_width), attn_ys.reshape(db * tn, attn_width),
                                 mem_ys.reshape(db * tn, mem_width), state_conv[l], db, tn)
    pool_st_s = _last_rows(state_pool[l], us3, POOL_CTX)

    hd = HEAD_DIM
    return (y_p.reshape(bsz, seq, d), y_s.reshape(db, tn, d),
            k.reshape(1, bsz, seq, n_heads, hd), v.reshape(1, bsz, seq, n_heads, hd),
            mk.reshape(1, bsz, n_mem, MEM_HEADS, hd), mv.reshape(1, bsz, n_mem, MEM_HEADS, hd),
            pool_st_p[None], conv_st_p[None],
            ks.reshape(1, db, tn, n_heads, hd), vs.reshape(1, db, tn, n_heads, hd),
            pool_st_s[None], conv_st_s[None])
```

```python
import functools

import jax
import jax.numpy as jnp
from jax import lax
from jax.experimental import pallas as pl
from jax.experimental.pallas import tpu as pltpu

HEAD_DIM = 128
POOL_WINDOWS = (2, 4, 8, 16)
POOL_CTX = max(POOL_WINDOWS) - 1
POOL_PAD = 16
MEM_HEADS = 4
MOBA_BLOCK = 256
MOBA_TOPK = 3
PAGE_SIZE = 128
CONV_WIDTH = 3
CONV_CTX = CONV_WIDTH - 1
CONV_PAD = 8
NORM_EPS = 1e-6
NEG_INF = -1e30

VMEM_LIMIT_BYTES = 56 * 1024 * 1024

BF16 = jnp.bfloat16
F32 = jnp.float32


def _params(*semantics):
    return pltpu.CompilerParams(dimension_semantics=semantics, vmem_limit_bytes=VMEM_LIMIT_BYTES)


def _whole_vmem():
    return pl.BlockSpec(memory_space=pltpu.VMEM)


def _row_tile(n, want):
    if n <= want:
        return n
    for t in range(want - want % 8, 0, -8):
        if n % t == 0:
            return t
    raise ValueError((n, want))


def _dot(a, b):
    return jnp.dot(a, b, preferred_element_type=F32)


def _dot_nt(a, b, precision=None):
    return lax.dot_general(a, b, (((1,), (1,)), ((), ())), precision=precision, preferred_element_type=F32)


def _rms_scale(x):
    return lax.rsqrt(jnp.mean(x * x, axis=-1, keepdims=True) + NORM_EPS)


FLAT, HEADS, HEADS_MEAN = "flat", "heads", "heads_mean"
_N_OUTS = {FLAT: 1, HEADS: 2, HEADS_MEAN: 3}


def _norm_proj_kernel(x_ref, nw_ref, w_ref, *rest, segs, n_gain, chunk):
    gain_refs = rest[:n_gain]
    out_refs = list(rest[n_gain:])
    tm = x_ref.shape[0]
    x = x_ref[...]
    xn = (x * _rms_scale(x) * nw_ref[...]).astype(BF16)
    for start, width, gi, mode in segs:
        outs = [out_refs.pop(0) for _ in range(_N_OUTS[mode])]
        heads = width // HEAD_DIM
        for c0 in range(0, width, chunk):
            z = _dot(xn, w_ref[:, start + c0:start + c0 + chunk])
            for h0 in range(0, chunk, HEAD_DIM):
                zh = z[:, h0:h0 + HEAD_DIM]
                if gi is not None:
                    zh = zh * _rms_scale(zh) * gain_refs[gi][...]
                col = slice(c0 + h0, c0 + h0 + HEAD_DIM)
                if mode == FLAT:
                    outs[0][:, col] = zh
                    continue
                outs[0][pl.ds((c0 + h0) // HEAD_DIM, tm, stride=heads), :] = zh
                outs[1][:, col] = zh.astype(BF16)
                if mode == HEADS_MEAN:
                    for r in range(tm // MOBA_BLOCK):
                        outs[2][0, r:r + 1, col] = jnp.mean(zh[r * MOBA_BLOCK:(r + 1) * MOBA_BLOCK], axis=0,
                                                            keepdims=True)


def _norm_proj(x2d, norm_w, w_bf16, gains, segs, tm_want=512):
    n, d = x2d.shape
    tm = _row_tile(n, tm_want)
    chunk = 512
    assert all(w % chunk == 0 for _, w, _, _ in segs)
    in_specs = [pl.BlockSpec((tm, d), lambda i: (i, 0)), _whole_vmem(), _whole_vmem()]
    in_specs += [_whole_vmem() for _ in gains]
    out_specs, out_shape = [], []
    for _, w, _, mode in segs:
        heads = w // HEAD_DIM
        if mode == FLAT:
            out_specs.append(pl.BlockSpec((tm, w), lambda i: (i, 0)))
            out_shape.append(jax.ShapeDtypeStruct((n, w), F32))
            continue
        out_specs.append(pl.BlockSpec((tm * heads, HEAD_DIM), lambda i: (i, 0)))
        out_shape.append(jax.ShapeDtypeStruct((n * heads, HEAD_DIM), F32))
        out_specs.append(pl.BlockSpec((tm, w), lambda i: (i, 0)))
        out_shape.append(jax.ShapeDtypeStruct((n, w), BF16))
        if mode == HEADS_MEAN:
            assert tm % MOBA_BLOCK == 0
            out_specs.append(pl.BlockSpec((1, tm // MOBA_BLOCK, w), lambda i: (i, 0, 0)))
            out_shape.append(jax.ShapeDtypeStruct((n // tm, tm // MOBA_BLOCK, w), F32))
    kern = functools.partial(_norm_proj_kernel, segs=tuple(segs), n_gain=len(gains), chunk=chunk)
    return pl.pallas_call(
        kern, out_shape=out_shape, grid=(n // tm,), in_specs=in_specs, out_specs=out_specs,
        compiler_params=_params("arbitrary"), name="norm_proj",
    )(x2d, norm_w.reshape(1, d), w_bf16, *[g.reshape(1, HEAD_DIM) for g in gains])


def _pool_kernel(u_ref, prev_ref, pw_ref, ps_ref, y_ref, ext_ref, *, tt, pos0):
    j = pl.program_id(1)

    @pl.when(j == 0)
    def _():
        ext_ref[0:POOL_PAD, :] = prev_ref[0]

    @pl.when(j > 0)
    def _():
        ext_ref[0:POOL_PAD, :] = ext_ref[tt:tt + POOL_PAD, :]

    u = u_ref[0]
    ext_ref[POOL_PAD:POOL_PAD + tt, :] = u
    pos = pos0 + j * tt + lax.broadcasted_iota(jnp.int32, (tt, 1), 0)
    for g, w in enumerate(POOL_WINDOWS):
        c = slice(g * HEAD_DIM, (g + 1) * HEAD_DIM)
        ug = u[:, c]
        acc = ug
        for s in range(1, w):
            acc = acc + ext_ref[POOL_PAD - s:POOL_PAD - s + tt, c]
        cnt = jnp.minimum(pos + 1, w).astype(F32)
        diff = acc / cnt - ug
        y_ref[0, :, c] = _dot(diff.astype(BF16), pw_ref[g]) * ps_ref[:, c]


def _pool_mix(u, prev, pos0, pool_w_bf16, pool_scale, tt_want=1024):
    b, t, c = u.shape
    assert c == HEAD_DIM * len(POOL_WINDOWS)
    tt = _row_tile(t, tt_want)
    assert tt == t or tt >= POOL_PAD
    kern = functools.partial(_pool_kernel, tt=tt, pos0=pos0)
    return pl.pallas_call(
        kern, out_shape=jax.ShapeDtypeStruct((b, t, c), F32), grid=(b, t // tt),
        in_specs=[pl.BlockSpec((1, tt, c), lambda i, j: (i, j, 0)),
                  pl.BlockSpec((1, POOL_PAD, c), lambda i, j: (i, 0, 0)),
                  _whole_vmem(), _whole_vmem()],
        out_specs=pl.BlockSpec((1, tt, c), lambda i, j: (i, j, 0)),
        scratch_shapes=[pltpu.VMEM((POOL_PAD + tt, c), F32)],
        compiler_params=_params("arbitrary", "arbitrary"), name="pool_mix",
    )(u, prev, pool_w_bf16, pool_scale.reshape(1, c))


def _moba_prompt_kernel(slope_ref, q_ref, k_ref, v_ref, km_ref, o_ref, *, t_len):
    L = MOBA_BLOCK
    nblk = t_len // L
    scale = HEAD_DIM ** -0.5
    slope = slope_ref[0, 0:1, 0:1]
    q = q_ref[0]
    kb = k_ref[0]
    vb = v_ref[0]
    n_sel = min(MOBA_TOPK, nblk - 1)
    assert nblk <= 8 and nblk + 3 <= HEAD_DIM
    F_LO, F_HI, F_ONE = nblk, nblk + 1, nblk + 2
    row = lax.broadcasted_iota(jnp.int32, (t_len, HEAD_DIM), 0)
    lane = lax.broadcasted_iota(jnp.int32, (t_len, HEAD_DIM), 1)
    blk = row // L
    blk_f = blk.astype(F32)
    k_feat = jnp.where(lane == blk, 1.0, 0.0)
    k_feat = jnp.where(lane == F_LO, (row - blk * L).astype(F32), k_feat)
    k_feat = jnp.where(lane == F_HI, blk_f, k_feat)
    k_feat = jnp.where(lane == F_ONE, 1.0, k_feat)
    k_aug = jnp.concatenate([kb, k_feat.astype(BF16)], axis=-1)

    q_feat = jnp.zeros((t_len, HEAD_DIM), F32)
    if n_sel > 0:
        km = jnp.concatenate([km_ref[0], jnp.zeros((HEAD_DIM - nblk, HEAD_DIM), F32)], axis=0)
        km_hi = km.astype(BF16)
        km_lo = (km - km_hi.astype(F32)).astype(BF16)
        r0 = L
        qr = q[r0:]
        q_hi = qr.astype(BF16)
        q_lo = (qr - q_hi.astype(F32)).astype(BF16)
        gate = (_dot_nt(jnp.concatenate([q_hi, q_lo], axis=-1), jnp.concatenate([km_hi, km_hi], axis=-1))
                + _dot_nt(q_hi, km_lo))
        g = gate.T[0:8]
        sub = lax.broadcasted_iota(jnp.int32, g.shape, 0)
        n_full = (r0 + lax.broadcasted_iota(jnp.int32, g.shape, 1)) // L
        gm = jnp.where(sub < n_full, g, NEG_INF)
        pen = jnp.zeros(g.shape, F32)
        for j in range(nblk - 1):
            gj = gm[j:j + 1, :]
            beats = (gm > gj) | ((gm == gj) & (sub < j))
            rank = jnp.sum(beats.astype(F32), axis=0, keepdims=True)
            sel = (rank < n_sel) & (n_full > j)
            pen = jnp.where(sub == j, jnp.where(sel, 0.0, NEG_INF), pen)
        pen = jnp.where(sub < n_full, pen, 0.0)
        pen_t = jnp.concatenate([pen, jnp.zeros((HEAD_DIM - 8, pen.shape[1]), F32)], axis=0).T
        q_feat = jnp.concatenate([jnp.zeros((r0, HEAD_DIM), F32), pen_t], axis=0)
    q_feat = jnp.where(lane == F_LO, slope, q_feat)
    q_feat = jnp.where(lane == F_HI, slope * float(L), q_feat)
    q_feat = jnp.where(lane == F_ONE, -slope * float(L) * blk_f, q_feat)
    q_aug = jnp.concatenate([(q * scale).astype(BF16), q_feat.astype(BF16)], axis=-1)

    causal = (lax.broadcasted_iota(jnp.int32, (L, L), 0) >= lax.broadcasted_iota(jnp.int32, (L, L), 1))
    def scores(n):
        return _dot_nt(q_aug[n * L:(n + 1) * L], k_aug[0:(n + 1) * L])

    s_next = scores(0)
    for n in range(nblk):
        s = s_next
        if n + 1 < nblk:
            s_next = scores(n + 1)
        s_own = jnp.where(causal, s[:, n * L:], NEG_INF)
        m = jnp.max(s_own, axis=-1, keepdims=True)
        if n > 0:
            s_past = s[:, :n * L]
            m = jnp.maximum(m, jnp.max(s_past, axis=-1, keepdims=True))
        p_own = jnp.exp(s_own - m)
        l = jnp.sum(p_own, axis=-1, keepdims=True)
        acc = _dot(p_own.astype(BF16), vb[n * L:(n + 1) * L])
        if n > 0:
            p_past = jnp.exp(s_past - m)
            l = l + jnp.sum(p_past, axis=-1, keepdims=True)
            acc = acc + _dot(p_past.astype(BF16), vb[:n * L])
        o_ref[0, n * L:(n + 1) * L, :] = acc / l


def _moba_prompt(q, k_bf16, v_bf16, kmean, slopes_tile, n_heads):
    b, t, w = q.shape
    assert t % MOBA_BLOCK == 0 and w == n_heads * HEAD_DIM
    nblk = t // MOBA_BLOCK
    spec = pl.BlockSpec((1, t, HEAD_DIM), lambda i, h: (i, 0, h))
    kern = functools.partial(_moba_prompt_kernel, t_len=t)
    return pl.pallas_call(
        kern, out_shape=jax.ShapeDtypeStruct((b, t, w), F32), grid=(b, n_heads),
        in_specs=[pl.BlockSpec((1, 8, HEAD_DIM), lambda i, h: (h, 0, 0)), spec, spec, spec,
                  pl.BlockSpec((1, nblk, HEAD_DIM), lambda i, h: (i, 0, h))],
        out_specs=spec,
        compiler_params=_params("arbitrary", "arbitrary"), name="moba_prompt",
    )(slopes_tile, q, k_bf16, v_bf16, kmean)


def _mem_attn_kernel(q_ref, k_ref, v_ref, o_ref, *, n_heads, head_rows):
    scale = HEAD_DIM ** -0.5
    n_mem = k_ref.shape[1] // n_heads if head_rows else k_ref.shape[1]
    for h in range(n_heads):
        c = slice(h * HEAD_DIM, (h + 1) * HEAD_DIM)
        if head_rows:
            kh = k_ref[0, pl.ds(h, n_mem, stride=n_heads), :].astype(BF16)
            vh = v_ref[0, pl.ds(h, n_mem, stride=n_heads), :].astype(BF16)
        else:
            kh = k_ref[0, :, c].astype(BF16)
            vh = v_ref[0, :, c].astype(BF16)
        s = _dot_nt(q_ref[0, :, c].astype(BF16), kh) * scale
        m = jnp.max(s, axis=-1, keepdims=True)
        p = jnp.exp(s - m)
        l = jnp.sum(p, axis=-1, keepdims=True)
        o_ref[0, :, c] = _dot(p.astype(BF16), vh) / l


def _mem_attend(mq, mk, mv, n_heads, head_rows, tq_want=1024):
    b, t, w = mq.shape
    tq = _row_tile(t, tq_want)
    qspec = pl.BlockSpec((1, tq, w), lambda i, j: (i, j, 0))
    kspec = pl.BlockSpec((1,) + mk.shape[1:], lambda i, j: (i, 0, 0))
    return pl.pallas_call(
        functools.partial(_mem_attn_kernel, n_heads=n_heads, head_rows=head_rows),
        out_shape=jax.ShapeDtypeStruct((b, t, w), F32),
        grid=(b, t // tq), in_specs=[qspec, kspec, kspec], out_specs=qspec,
        compiler_params=_params("arbitrary", "arbitrary"), name="mem_attend",
    )(mq, mk, mv)


def _out_proj_kernel(x_ref, p_ref, a_ref, m_ref, w_ref, nw_ref, x1_ref, h_ref, mix_ref):
    o1 = p_ref.shape[1]
    o2 = o1 + a_ref.shape[1]
    mix_ref[:, 0:o1] = p_ref[...].astype(BF16)
    mix_ref[:, o1:o2] = a_ref[...].astype(BF16)
    mix_ref[:, o2:] = m_ref[...].astype(BF16)
    x1 = x_ref[...] + _dot(mix_ref[...], w_ref[...])
    x1_ref[...] = x1
    h_ref[...] = (x1 * _rms_scale(x1) * nw_ref[...]).astype(BF16)


def _out_proj(x2d, pool_y, attn_y, mem_y, w_out_bf16, norm_ffn, tm_want=512):
    n, d = x2d.shape
    tm = _row_tile(n, tm_want)
    widths = (pool_y.shape[1], attn_y.shape[1], mem_y.shape[1])
    assert sum(widths) == w_out_bf16.shape[0]
    row = lambda w: pl.BlockSpec((tm, w), lambda i: (i, 0))
    return pl.pallas_call(
        _out_proj_kernel,
        out_shape=[jax.ShapeDtypeStruct((n, d), F32), jax.ShapeDtypeStruct((n, d), BF16)],
        grid=(n // tm,),
        in_specs=[row(d), row(widths[0]), row(widths[1]), row(widths[2]), _whole_vmem(), _whole_vmem()],
        out_specs=[row(d), row(d)],
        scratch_shapes=[pltpu.VMEM((tm, sum(widths)), BF16)],
        compiler_params=_params("arbitrary"), name="out_proj",
    )(x2d, pool_y, attn_y, mem_y, w_out_bf16, norm_ffn.reshape(1, d))


def _ffn_up_kernel(h_ref, wg_ref, wv_ref, cwg_ref, cwv_ref, cbg_ref, cbv_ref, pg_ref, pv_ref,
                   a_ref, sg_ref, sv_ref, eg_ref, ev_ref, *, nseq, tt, fc, sub):
    j = pl.program_id(2)
    lo = CONV_PAD - CONV_CTX

    @pl.when(j == 0)
    def _():
        eg_ref[:, lo:CONV_PAD, :] = pg_ref[...]
        ev_ref[:, lo:CONV_PAD, :] = pv_ref[...]

    @pl.when(j > 0)
    def _():
        eg_ref[:, lo:CONV_PAD, :] = eg_ref[:, tt + lo:tt + CONV_PAD, :]
        ev_ref[:, lo:CONV_PAD, :] = ev_ref[:, tt + lo:tt + CONV_PAD, :]

    def conv(hm, r0, w_ref, cw_ref, cb_ref, ext_ref, state_ref):
        u = _dot(hm, w_ref[...]).reshape(nseq, sub, fc)
        ext_ref[:, CONV_PAD + r0:CONV_PAD + r0 + sub, :] = u
        if r0 + sub == tt:
            state_ref[...] = u[:, sub - CONV_CTX:, :]
        out = cb_ref[...].reshape(1, 1, fc)
        for kk in range(CONV_WIDTH):
            out = out + ext_ref[:, lo + r0 + kk:lo + r0 + kk + sub, :] * cw_ref[kk:kk + 1, :].reshape(1, 1, fc)
        return out

    for r0 in range(0, tt, sub):
        rows = slice(r0 * nseq, (r0 + sub) * nseq)
        hm = h_ref[rows, :]
        g = conv(hm, r0, wg_ref, cwg_ref, cbg_ref, eg_ref, sg_ref)
        val = conv(hm, r0, wv_ref, cwv_ref, cbv_ref, ev_ref, sv_ref)
        act = g * (1.0 / (1.0 + jnp.exp(-g))) * val
        a_ref[rows, :] = act.reshape(nseq * sub, fc).astype(BF16)


def _ffn_up_scan_kernel(pt_ref, *refs, n_scan, n_heads, **kw):
    del pt_ref
    n_ffn_in = 9
    page_refs = refs[n_ffn_in:n_ffn_in + n_scan]
    km_ref = refs[n_ffn_in + n_scan + 3]
    ppb = MOBA_BLOCK // PAGE_SIZE
    for n in range(n_scan // ppb):
        s = page_refs[n * ppb][...]
        for p in range(1, ppb):
            s = s + page_refs[n * ppb + p][...]
        km_ref[n * n_heads:(n + 1) * n_heads, :] = (
            jnp.sum(s.reshape(PAGE_SIZE, n_heads, HEAD_DIM), axis=0) * (1.0 / MOBA_BLOCK))
    _ffn_up_kernel(*refs[:n_ffn_in], *refs[n_ffn_in + n_scan:n_ffn_in + n_scan + 3], *refs[n_ffn_in + n_scan + 4:],
                   **kw)


def _ffn_up(h2d, nseq, tt, w_up_bf16, conv_w, conv_b, prev, fc=512, scan=None):
    n, d = h2d.shape
    f2 = w_up_bf16.shape[1]
    f = f2 // 2
    assert f % fc == 0
    nfc = f // fc
    n_seqs = prev.shape[0]
    seq_len = n // n_seqs
    assert seq_len % tt == 0 and (nseq == 1 or tt == seq_len) and n_seqs % nseq == 0
    tiles_per_seq = seq_len // tt
    rows = nseq * tt
    sub = _row_tile(tt, 256) if nseq == 1 else tt
    grid = (n_seqs // nseq, nfc, tiles_per_seq)
    kw = dict(nseq=nseq, tt=tt, fc=fc, sub=sub)
    wspec = lambda off: pl.BlockSpec((d, fc), lambda i, c, j, *_: (0, c + off))
    vspec = lambda r, off: pl.BlockSpec((r, fc), lambda i, c, j, *_: (0, c + off))
    pspec = lambda off: pl.BlockSpec((nseq, CONV_CTX, fc), lambda i, c, j, *_: (i, 0, c + off))
    sspec = pl.BlockSpec((nseq, CONV_CTX, fc), lambda i, c, j, *_: (i, 0, c))
    in_specs = [pl.BlockSpec((rows, d), lambda i, c, j, *_: (i * tiles_per_seq + j, 0)),
                wspec(0), wspec(nfc), vspec(CONV_WIDTH, 0), vspec(CONV_WIDTH, nfc),
                vspec(1, 0), vspec(1, nfc), pspec(0), pspec(nfc)]
    out_specs = [pl.BlockSpec((rows, fc), lambda i, c, j, *_: (i * tiles_per_seq + j, c)), sspec, sspec]
    out_shape = [jax.ShapeDtypeStruct((n, f), BF16),
                 jax.ShapeDtypeStruct((n_seqs, CONV_CTX, f), F32),
                 jax.ShapeDtypeStruct((n_seqs, CONV_CTX, f), F32)]
    scratch = [pltpu.VMEM((nseq, CONV_PAD + tt, fc), F32), pltpu.VMEM((nseq, CONV_PAD + tt, fc), F32)]
    args = (h2d, w_up_bf16, w_up_bf16, conv_w, conv_w, conv_b.reshape(1, f2), conv_b.reshape(1, f2), prev, prev)
    if scan is None:
        return pl.pallas_call(
            functools.partial(_ffn_up_kernel, **kw), out_shape=out_shape, grid=grid, in_specs=in_specs,
            out_specs=out_specs, scratch_shapes=scratch,
            compiler_params=_params("arbitrary", "arbitrary", "arbitrary"), name="ffn_up")(*args)

    cache_flat, pages, n_heads = scan
    ppb = MOBA_BLOCK // PAGE_SIZE
    n_steps = grid[0] * grid[1] * grid[2]
    total = pages.shape[0]
    assert total % ppb == 0
    n_scan = -(-total // (n_steps * ppb)) * ppb
    page_rows = PAGE_SIZE * n_heads
    km_rows = n_scan // ppb * n_heads

    def step_of(i, c, j):
        return (i * nfc + c) * tiles_per_seq + j

    def page_spec(p):
        return pl.BlockSpec((page_rows, HEAD_DIM),
                            lambda i, c, j, pt: (pt[jnp.minimum(step_of(i, c, j) * n_scan + p, total - 1)], 0))

    grid_spec = pltpu.PrefetchScalarGridSpec(
        num_scalar_prefetch=1, grid=grid,
        in_specs=in_specs + [page_spec(p) for p in range(n_scan)],
        out_specs=out_specs + [pl.BlockSpec((km_rows, HEAD_DIM), lambda i, c, j, pt: (step_of(i, c, j), 0))],
        scratch_shapes=scratch)
    act, sg, sv, km = pl.pallas_call(
        functools.partial(_ffn_up_scan_kernel, n_scan=n_scan, n_heads=n_heads, **kw),
        out_shape=out_shape + [jax.ShapeDtypeStruct((n_steps * km_rows, HEAD_DIM), F32)], grid_spec=grid_spec,
        compiler_params=_params("arbitrary", "arbitrary", "arbitrary"), name="ffn_up_scan",
    )(pages, *args, *([cache_flat] * n_scan))
    return act, sg, sv, km[:total // ppb * n_heads]


def _ffn_down_kernel(a_ref, x_ref, w_ref, y_ref):
    y_ref[...] = x_ref[...] + _dot(a_ref[...], w_ref[...])


def _ffn_down(act, x1, w_down_bf16, tm_want=256):
    n, f = act.shape
    d = x1.shape[1]
    tm = _row_tile(n, tm_want)
    return pl.pallas_call(
        _ffn_down_kernel, out_shape=jax.ShapeDtypeStruct((n, d), F32), grid=(n // tm,),
        in_specs=[pl.BlockSpec((tm, f), lambda i: (i, 0)), pl.BlockSpec((tm, d), lambda i: (i, 0)), _whole_vmem()],
        out_specs=pl.BlockSpec((tm, d), lambda i: (i, 0)),
        compiler_params=_params("arbitrary"), name="ffn_down",
    )(act, x1, w_down_bf16)


TOPK_LANES = 4


def _gate_topk_kernel(q_ref, km_ref, idx_ref, *, n_heads, n_sel):
    q = q_ref[0]
    tn = q.shape[0]
    n_full = km_ref.shape[1] // n_heads
    lane = lax.broadcasted_iota(jnp.int32, (tn, n_full), 1).astype(F32)
    out_lane = lax.broadcasted_iota(jnp.int32, (tn, HEAD_DIM), 1)
    out = jnp.zeros((tn, HEAD_DIM), F32)
    for h in range(n_heads):
        c = slice(h * HEAD_DIM, (h + 1) * HEAD_DIM)
        km = km_ref[0, pl.ds(h, n_full, stride=n_heads), :]
        g = _dot_nt(q[:, c], km, precision=lax.Precision.HIGHEST)
        for r in range(n_sel):
            m = jnp.max(g, axis=-1, keepdims=True)
            idx = jnp.min(jnp.where(g == m, lane, float(n_full)), axis=-1, keepdims=True)
            out = jnp.where(out_lane == h * TOPK_LANES + r, idx, out)
            g = jnp.where(lane == idx, -jnp.inf, g)
    idx_ref[0] = out.astype(jnp.int32)


def _gate_topk(q_s, kmean, n_heads, n_sel):
    db, tn, w = q_s.shape
    rows = kmean.shape[1]
    assert n_heads * TOPK_LANES <= HEAD_DIM and n_sel <= TOPK_LANES
    kern = functools.partial(_gate_topk_kernel, n_heads=n_heads, n_sel=n_sel)
    return pl.pallas_call(
        kern, out_shape=jax.ShapeDtypeStruct((db, tn, HEAD_DIM), jnp.int32), grid=(db,),
        in_specs=[pl.BlockSpec((1, tn, w), lambda b: (b, 0, 0)),
                  pl.BlockSpec((1, rows, HEAD_DIM), lambda b: (b, 0, 0))],
        out_specs=pl.BlockSpec((1, tn, HEAD_DIM), lambda b: (b, 0, 0)),
        compiler_params=_params("arbitrary"), name="gate_topk",
    )(q_s, kmean)


def _moba_sample_kernel(pt_ref, idx_ref, slope_ref, q_ref, kn_ref, vn_ref, ck_ref, cv_ref, o_ref,
                        kbuf, vbuf, sem, *, tn, n_sel, n_heads, past):
    L = MOBA_BLOCK
    ppb = L // PAGE_SIZE
    n_slots = tn * n_sel * ppb
    b = pl.program_id(0)
    h = pl.program_id(1)
    step = b * n_heads + h
    n_steps = pl.num_programs(0) * n_heads
    cur = lax.rem(step, 2)
    scale = HEAD_DIM ** -0.5

    def block_of(bb, hh, t, r):
        return idx_ref[((bb * n_heads + hh) * tn + t) * n_sel + r]

    def gathers(bb, hh, buf_slot):
        out = []
        for t in range(tn):
            for r in range(n_sel):
                blk = block_of(bb, hh, t, r)
                for p in range(ppb):
                    page = pt_ref[bb, blk * ppb + p]
                    i = (t * n_sel + r) * ppb + p
                    out.append(pltpu.make_async_copy(ck_ref.at[page, :, hh, :], kbuf.at[buf_slot, i],
                                                     sem.at[0, buf_slot]))
                    out.append(pltpu.make_async_copy(cv_ref.at[page, :, hh, :], vbuf.at[buf_slot, i],
                                                     sem.at[1, buf_slot]))
        return out

    @pl.when(step == 0)
    def _():
        for c in gathers(b, h, cur):
            c.start()

    @pl.when(step + 1 < n_steps)
    def _():
        nxt = step + 1
        for c in gathers(nxt // n_heads, lax.rem(nxt, n_heads), 1 - cur):
            c.start()

    for c in gathers(b, h, cur):
        c.wait()

    slope = slope_ref[0, 0:1, 0:1]
    qb = q_ref[0].astype(BF16)
    n_keys = n_slots * PAGE_SIZE
    s = _dot_nt(qb, kbuf[cur].reshape(n_keys, HEAD_DIM).astype(BF16)) * scale
    lane = lax.broadcasted_iota(jnp.int32, (tn, PAGE_SIZE), 1)
    k_pos = jnp.concatenate(
        [block_of(b, h, t, r) * L + p * PAGE_SIZE + lane for t in range(tn) for r in range(n_sel) for p in range(ppb)],
        axis=1)
    row = lax.broadcasted_iota(jnp.int32, (tn, n_keys), 0)
    col = lax.broadcasted_iota(jnp.int32, (tn, n_keys), 1)
    per_query = n_sel * L
    mine = (col >= row * per_query) & (col < row * per_query + per_query)
    s = jnp.where(mine, s - slope * (past + row - k_pos).astype(F32), NEG_INF)
    row_n = lax.broadcasted_iota(jnp.int32, (tn, tn), 0)
    lane_n = lax.broadcasted_iota(jnp.int32, (tn, tn), 1)
    s_own = _dot_nt(qb, kn_ref[0]) * scale - slope * (row_n - lane_n).astype(F32)
    s_own = jnp.where(lane_n <= row_n, s_own, NEG_INF)
    m = jnp.maximum(jnp.max(s, axis=-1, keepdims=True), jnp.max(s_own, axis=-1, keepdims=True))
    p = jnp.exp(s - m)
    p_own = jnp.exp(s_own - m)
    l = jnp.sum(p, axis=-1, keepdims=True) + jnp.sum(p_own, axis=-1, keepdims=True)
    acc = _dot(p.astype(BF16), vbuf[cur].reshape(n_keys, HEAD_DIM).astype(BF16)) + _dot(p_own.astype(BF16), vn_ref[0])
    o_ref[0] = acc / l


def _moba_sample(q_s, k_new_bf16, v_new_bf16, cache_k4, cache_v4, page_table, idx, slopes_tile, n_heads):
    db, tn, w = q_s.shape
    n_sel = idx.shape[-1]
    ppb = MOBA_BLOCK // PAGE_SIZE
    past = page_table.shape[1] * PAGE_SIZE
    n_slots = tn * n_sel * ppb
    new_spec = pl.BlockSpec((1, tn, HEAD_DIM), lambda b, h, pt, ix: (b, 0, h))
    hbm = pl.BlockSpec(memory_space=pl.ANY)
    grid_spec = pltpu.PrefetchScalarGridSpec(
        num_scalar_prefetch=2, grid=(db, n_heads),
        in_specs=[pl.BlockSpec((1, 8, HEAD_DIM), lambda b, h, pt, ix: (h, 0, 0)), new_spec, new_spec, new_spec,
                  hbm, hbm],
        out_specs=new_spec,
        scratch_shapes=[pltpu.VMEM((2, n_slots, PAGE_SIZE, HEAD_DIM), F32),
                        pltpu.VMEM((2, n_slots, PAGE_SIZE, HEAD_DIM), F32),
                        pltpu.SemaphoreType.DMA((2, 2))])
    kern = functools.partial(_moba_sample_kernel, tn=tn, n_sel=n_sel, n_heads=n_heads, past=past)
    return pl.pallas_call(
        kern, out_shape=jax.ShapeDtypeStruct((db, tn, w), F32), grid_spec=grid_spec,
        compiler_params=_params("arbitrary", "arbitrary"), name="moba_sample",
    )(page_table, idx.reshape(-1), slopes_tile, q_s, k_new_bf16, v_new_bf16, cache_k4, cache_v4)


def _last_rows(prev, new, n):
    t = new.shape[1]
    if t >= n:
        return new[:, t - n:]
    return jnp.concatenate([prev[:, prev.shape[1] - (n - t):], new], axis=1)


def kernel(x_prompt, x_sample, cache_k, cache_v, cache_mem_k, cache_mem_v, state_pool, state_conv, page_table, mem_prompt, norm_mix, w_in, pool_w, pool_scale, q_gain, k_gain, norm_mem, w_mem_kv, mq_gain, mk_gain, w_out, norm_ffn, w_up, conv_w, conv_b, w_down):
    depth = w_in.shape[0]
    assert depth == 1
    bsz, seq, d = x_prompt.shape
    db, tn, _ = x_sample.shape
    pool_width = pool_w.shape[1] * pool_w.shape[2]
    n_heads = cache_k.shape[3]
    attn_width = n_heads * HEAD_DIM
    mem_width = MEM_HEADS * HEAD_DIM
    n_mem = mem_prompt.shape[1]
    n_pool = cache_k.shape[1]
    n_pages = page_table.shape[1]
    past = n_pages * PAGE_SIZE
    n_full = past // MOBA_BLOCK
    assert past % MOBA_BLOCK == 0
    f2 = w_up.shape[2]

    slopes = jnp.exp2(-8.0 * jnp.arange(1, n_heads + 1, dtype=F32) / n_heads)
    slopes_tile = jnp.broadcast_to(slopes[:, None, None], (n_heads, 8, HEAD_DIM))

    l = 0
    w_in_b = w_in[l].astype(BF16)
    w_out_b = w_out[l].astype(BF16)
    w_up_b = w_up[l].astype(BF16)
    w_down_b = w_down[l].astype(BF16)
    w_mem_b = w_mem_kv[l].astype(BF16)
    pool_w_b = pool_w[l].astype(BF16)

    o1 = pool_width
    o2 = o1 + attn_width
    o3 = o2 + attn_width
    o4 = o3 + attn_width
    def in_segs(k_mode):
        return [(0, pool_width, None, FLAT), (o1, attn_width, 0, FLAT), (o2, attn_width, 1, k_mode),
                (o3, attn_width, None, HEADS), (o4, mem_width, 2, FLAT)]

    in_gains = [q_gain[l], k_gain[l], mq_gain[l]]

    def mix_and_ffn(x2d, pool_y, attn_y, mem_y, conv_prev, nseq, tt, scan=None):
        x1, h = _out_proj(x2d, pool_y, attn_y, mem_y, w_out_b, norm_ffn[l])
        act, sg, sv, *extra = _ffn_up(h, nseq, tt, w_up_b, conv_w[l], conv_b[l], conv_prev, scan=scan)
        y = _ffn_down(act, x1, w_down_b)
        return (y, jnp.concatenate([sg, sv], axis=-1), *extra)

    xp2 = x_prompt.reshape(bsz * seq, d)
    u, q, k, k_b, k_mean, v, v_b, mq = _norm_proj(xp2, norm_mix[l], w_in_b, in_gains, in_segs(HEADS_MEAN))
    u3 = u.reshape(bsz, seq, pool_width)
    pool_y = _pool_mix(u3, jnp.zeros((bsz, POOL_PAD, pool_width), F32), 0, pool_w_b, pool_scale[l])
    attn_y = _moba_prompt(q.reshape(bsz, seq, attn_width), k_b.reshape(bsz, seq, attn_width),
                          v_b.reshape(bsz, seq, attn_width), k_mean.reshape(bsz, seq // MOBA_BLOCK, attn_width),
                          slopes_tile, n_heads)
    mk, mk_b, mv, mv_b = _norm_proj(mem_prompt.reshape(bsz * n_mem, d), norm_mem[l], w_mem_b, [mk_gain[l]],
                                    [(0, mem_width, 0, HEADS), (mem_width, mem_width, None, HEADS)])
    mem_y = _mem_attend(mq.reshape(bsz, seq, mem_width), mk_b.reshape(bsz, n_mem, mem_width),
                        mv_b.reshape(bsz, n_mem, mem_width), MEM_HEADS, head_rows=False, tq_want=2048)
    cache_k_flat = cache_k[l].reshape(n_pool * PAGE_SIZE * n_heads, HEAD_DIM)
    y_p, conv_st_p, kmean = mix_and_ffn(
        xp2, pool_y.reshape(bsz * seq, pool_width), attn_y.reshape(bsz * seq, attn_width),
        mem_y.reshape(bsz * seq, mem_width), jnp.zeros((bsz, CONV_CTX, f2), F32), 1, _row_tile(seq, 1024),
        scan=(cache_k_flat, page_table.reshape(-1), n_heads))
    pool_st_p = _last_rows(jnp.zeros((bsz, POOL_CTX, pool_width), F32), u3, POOL_CTX)

    xs2 = x_sample.reshape(db * tn, d)
    us, qs, ks, ks_b, vs, vs_b, mqs = _norm_proj(xs2, norm_mix[l], w_in_b, in_gains, in_segs(HEADS))
    us3 = us.reshape(db, tn, pool_width)
    prev_pool = jnp.concatenate([jnp.zeros((db, POOL_PAD - POOL_CTX, pool_width), F32), state_pool[l]], axis=1)
    pool_ys = _pool_mix(us3, prev_pool, past, pool_w_b, pool_scale[l])
    qs3, ks3, vs3 = (a.reshape(db, tn, attn_width) for a in (qs, ks_b, vs_b))
    n_sel = min(MOBA_TOPK, n_full)
    assert n_sel > 0
    idx_lanes = _gate_topk(qs3, kmean.reshape(db, n_full * n_heads, HEAD_DIM), n_heads, n_sel)
    idx = idx_lanes[:, :, :n_heads * TOPK_LANES].reshape(db, tn, n_heads, TOPK_LANES)[..., :n_sel]
    attn_ys = _moba_sample(qs3, ks3, vs3, cache_k[l], cache_v[l], page_table, idx.transpose(0, 2, 1, 3),
                           slopes_tile, n_heads)
    mem_ys = _mem_attend(mqs.reshape(db, tn, mem_width), cache_mem_k[l].reshape(db, n_mem * MEM_HEADS, HEAD_DIM),
                         cache_mem_v[l].reshape(db, n_mem * MEM_HEADS, HEAD_DIM), MEM_HEADS, head_rows=True)
    y_s, conv_st_s = mix_and_ffn(xs2, pool_ys.reshape(db * tn, pool_width), attn_ys.reshape(db * tn, attn_width),
                                 mem_ys.reshape(db * tn, mem_width), state_conv[l], db, tn)
    pool_st_s = _last_rows(state_pool[l], us3, POOL_CTX)

    hd = HEAD_DIM
    return (y_p.reshape(bsz, seq, d), y_s.reshape(db, tn, d),
            k.reshape(1, bsz, seq, n_heads, hd), v.reshape(1, bsz, seq, n_heads, hd),
            mk.reshape(1, bsz, n_mem, MEM_HEADS, hd), mv.reshape(1, bsz, n_mem, MEM_HEADS, hd),
            pool_st_p[None], conv_st_p[None],
            ks.reshape(1, db, tn, n_heads, hd), vs.reshape(1, db, tn, n_heads, hd),
            pool_st_s[None], conv_st_s[None])
```

```python
import functools

import jax
import jax.numpy as jnp
from jax import lax
from jax.experimental import pallas as pl
from jax.experimental.pallas import tpu as pltpu

HEAD_DIM = 128
POOL_WINDOWS = (2, 4, 8, 16)
POOL_CTX = max(POOL_WINDOWS) - 1
POOL_PAD = 16
MEM_HEADS = 4
MOBA_BLOCK = 256
MOBA_TOPK = 3
PAGE_SIZE = 128
CONV_WIDTH = 3
CONV_CTX = CONV_WIDTH - 1
CONV_PAD = 8
NORM_EPS = 1e-6
NEG_INF = -1e30

VMEM_LIMIT_BYTES = 56 * 1024 * 1024

BF16 = jnp.bfloat16
F32 = jnp.float32


def _params(*semantics):
    return pltpu.CompilerParams(dimension_semantics=semantics, vmem_limit_bytes=VMEM_LIMIT_BYTES)


def _whole_vmem():
    return pl.BlockSpec(memory_space=pltpu.VMEM)


def _row_tile(n, want):
    if n <= want:
        return n
    for t in range(want - want % 8, 0, -8):
        if n % t == 0:
            return t
    raise ValueError((n, want))


def _dot(a, b):
    return jnp.dot(a, b, preferred_element_type=F32)


def _dot_nt(a, b, precision=None):
    return lax.dot_general(a, b, (((1,), (1,)), ((), ())), precision=precision, preferred_element_type=F32)


def _rms_scale(x):
    return lax.rsqrt(jnp.mean(x * x, axis=-1, keepdims=True) + NORM_EPS)


FLAT, HEADS, HEADS_MEAN = "flat", "heads", "heads_mean"
_N_OUTS = {FLAT: 1, HEADS: 2, HEADS_MEAN: 3}


def _norm_proj_kernel(x_ref, nw_ref, w_ref, *rest, segs, n_gain, chunk):
    gain_refs = rest[:n_gain]
    out_refs = list(rest[n_gain:])
    tm = x_ref.shape[0]
    x = x_ref[...]
    xn = (x * _rms_scale(x) * nw_ref[...]).astype(BF16)
    for start, width, gi, mode in segs:
        outs = [out_refs.pop(0) for _ in range(_N_OUTS[mode])]
        heads = width // HEAD_DIM
        for c0 in range(0, width, chunk):
            z = _dot(xn, w_ref[:, start + c0:start + c0 + chunk])
            for h0 in range(0, chunk, HEAD_DIM):
                zh = z[:, h0:h0 + HEAD_DIM]
                if gi is not None:
                    zh = zh * _rms_scale(zh) * gain_refs[gi][...]
                col = slice(c0 + h0, c0 + h0 + HEAD_DIM)
                if mode == FLAT:
                    outs[0][:, col] = zh
                    continue
                outs[0][pl.ds((c0 + h0) // HEAD_DIM, tm, stride=heads), :] = zh
                outs[1][:, col] = zh.astype(BF16)
                if mode == HEADS_MEAN:
                    for r in range(tm // MOBA_BLOCK):
                        outs[2][0, r:r + 1, col] = jnp.mean(zh[r * MOBA_BLOCK:(r + 1) * MOBA_BLOCK], axis=0,
                                                            keepdims=True)


def _norm_proj(x2d, norm_w, w_bf16, gains, segs, tm_want=512):
    n, d = x2d.shape
    tm = _row_tile(n, tm_want)
    chunk = 512
    assert all(w % chunk == 0 for _, w, _, _ in segs)
    in_specs = [pl.BlockSpec((tm, d), lambda i: (i, 0)), _whole_vmem(), _whole_vmem()]
    in_specs += [_whole_vmem() for _ in gains]
    out_specs, out_shape = [], []
    for _, w, _, mode in segs:
        heads = w // HEAD_DIM
        if mode == FLAT:
            out_specs.append(pl.BlockSpec((tm, w), lambda i: (i, 0)))
            out_shape.append(jax.ShapeDtypeStruct((n, w), F32))
            continue
        out_specs.append(pl.BlockSpec((tm * heads, HEAD_DIM), lambda i: (i, 0)))
        out_shape.append(jax.ShapeDtypeStruct((n * heads, HEAD_DIM), F32))
        out_specs.append(pl.BlockSpec((tm, w), lambda i: (i, 0)))
        out_shape.append(jax.ShapeDtypeStruct((n, w), BF16))
        if mode == HEADS_MEAN:
            assert tm % MOBA_BLOCK == 0
            out_specs.append(pl.BlockSpec((1, tm // MOBA_BLOCK, w), lambda i: (i, 0, 0)))
            out_shape.append(jax.ShapeDtypeStruct((n // tm, tm // MOBA_BLOCK, w), F32))
    kern = functools.partial(_norm_proj_kernel, segs=tuple(segs), n_gain=len(gains), chunk=chunk)
    return pl.pallas_call(
        kern, out_shape=out_shape, grid=(n // tm,), in_specs=in_specs, out_specs=out_specs,
        compiler_params=_params("arbitrary"), name="norm_proj",
    )(x2d, norm_w.reshape(1, d), w_bf16, *[g.reshape(1, HEAD_DIM) for g in gains])


def _pool_kernel(u_ref, prev_ref, pw_ref, ps_ref, y_ref, ext_ref, *, tt, pos0):
    j = pl.program_id(1)

    @pl.when(j == 0)
    def _():
        ext_ref[0:POOL_PAD, :] = prev_ref[0]

    @pl.when(j > 0)
    def _():
        ext_ref[0:POOL_PAD, :] = ext_ref[tt:tt + POOL_PAD, :]

    u = u_ref[0]
    ext_ref[POOL_PAD:POOL_PAD + tt, :] = u
    pos = pos0 + j * tt + lax.broadcasted_iota(jnp.int32, (tt, 1), 0)
    for g, w in enumerate(POOL_WINDOWS):
        c = slice(g * HEAD_DIM, (g + 1) * HEAD_DIM)
        ug = u[:, c]
        acc = ug
        for s in range(1, w):
            acc = acc + ext_ref[POOL_PAD - s:POOL_PAD - s + tt, c]
        cnt = jnp.minimum(pos + 1, w).astype(F32)
        diff = acc / cnt - ug
        y_ref[0, :, c] = _dot(diff.astype(BF16), pw_ref[g]) * ps_ref[:, c]


def _pool_mix(u, prev, pos0, pool_w_bf16, pool_scale, tt_want=1024):
    b, t, c = u.shape
    assert c == HEAD_DIM * len(POOL_WINDOWS)
    tt = _row_tile(t, tt_want)
    assert tt == t or tt >= POOL_PAD
    kern = functools.partial(_pool_kernel, tt=tt, pos0=pos0)
    return pl.pallas_call(
        kern, out_shape=jax.ShapeDtypeStruct((b, t, c), F32), grid=(b, t // tt),
        in_specs=[pl.BlockSpec((1, tt, c), lambda i, j: (i, j, 0)),
                  pl.BlockSpec((1, POOL_PAD, c), lambda i, j: (i, 0, 0)),
                  _whole_vmem(), _whole_vmem()],
        out_specs=pl.BlockSpec((1, tt, c), lambda i, j: (i, j, 0)),
        scratch_shapes=[pltpu.VMEM((POOL_PAD + tt, c), F32)],
        compiler_params=_params("arbitrary", "arbitrary"), name="pool_mix",
    )(u, prev, pool_w_bf16, pool_scale.reshape(1, c))


def _moba_prompt_kernel(slope_ref, q_ref, k_ref, v_ref, km_ref, o_ref, *, t_len):
    L = MOBA_BLOCK
    nblk = t_len // L
    scale = HEAD_DIM ** -0.5
    slope = slope_ref[0, 0:1, 0:1]
    q = q_ref[0]
    kb = k_ref[0]
    vb = v_ref[0]
    n_sel = min(MOBA_TOPK, nblk - 1)
    assert nblk <= 8 and nblk + 3 <= HEAD_DIM
    F_LO, F_HI, F_ONE = nblk, nblk + 1, nblk + 2
    row = lax.broadcasted_iota(jnp.int32, (t_len, HEAD_DIM), 0)
    lane = lax.broadcasted_iota(jnp.int32, (t_len, HEAD_DIM), 1)
    blk = row // L
    blk_f = blk.astype(F32)
    k_feat = jnp.where(lane == blk, 1.0, 0.0)
    k_feat = jnp.where(lane == F_LO, (row - blk * L).astype(F32), k_feat)
    k_feat = jnp.where(lane == F_HI, blk_f, k_feat)
    k_feat = jnp.where(lane == F_ONE, 1.0, k_feat)
    k_aug = jnp.concatenate([kb, k_feat.astype(BF16)], axis=-1)

    q_feat = jnp.zeros((t_len, HEAD_DIM), F32)
    if n_sel > 0:
        km = jnp.concatenate([km_ref[0], jnp.zeros((HEAD_DIM - nblk, HEAD_DIM), F32)], axis=0)
        km_hi = km.astype(BF16)
        km_lo = (km - km_hi.astype(F32)).astype(BF16)
        r0 = L
        qr = q[r0:]
        q_hi = qr.astype(BF16)
        q_lo = (qr - q_hi.astype(F32)).astype(BF16)
        gate = (_dot_nt(jnp.concatenate([q_hi, q_lo], axis=-1), jnp.concatenate([km_hi, km_hi], axis=-1))
                + _dot_nt(q_hi, km_lo))
        g = gate.T[0:8]
        sub = lax.broadcasted_iota(jnp.int32, g.shape, 0)
        n_full = (r0 + lax.broadcasted_iota(jnp.int32, g.shape, 1)) // L
        gm = jnp.where(sub < n_full, g, NEG_INF)
        pen = jnp.zeros(g.shape, F32)
        for j in range(nblk - 1):
            gj = gm[j:j + 1, :]
            beats = (gm > gj) | ((gm == gj) & (sub < j))
            rank = jnp.sum(beats.astype(F32), axis=0, keepdims=True)
            sel = (rank < n_sel) & (n_full > j)
            pen = jnp.where(sub == j, jnp.where(sel, 0.0, NEG_INF), pen)
        pen = jnp.where(sub < n_full, pen, 0.0)
        pen_t = jnp.concatenate([pen, jnp.zeros((HEAD_DIM - 8, pen.shape[1]), F32)], axis=0).T
        q_feat = jnp.concatenate([jnp.zeros((r0, HEAD_DIM), F32), pen_t], axis=0)
    q_feat = jnp.where(lane == F_LO, slope, q_feat)
    q_feat = jnp.where(lane == F_HI, slope * float(L), q_feat)
    q_feat = jnp.where(lane == F_ONE, -slope * float(L) * blk_f, q_feat)
    q_aug = jnp.concatenate([(q * scale).astype(BF16), q_feat.astype(BF16)], axis=-1)

    causal = (lax.broadcasted_iota(jnp.int32, (L, L), 0) >= lax.broadcasted_iota(jnp.int32, (L, L), 1))
    def scores(n):
        return _dot_nt(q_aug[n * L:(n + 1) * L], k_aug[0:(n + 1) * L])

    s_next = scores(0)
    for n in range(nblk):
        s = s_next
        if n + 1 < nblk:
            s_next = scores(n + 1)
        s_own = jnp.where(causal, s[:, n * L:], NEG_INF)
        m = jnp.max(s_own, axis=-1, keepdims=True)
        if n > 0:
            s_past = s[:, :n * L]
            m = jnp.maximum(m, jnp.max(s_past, axis=-1, keepdims=True))
        p_own = jnp.exp(s_own - m)
        l = jnp.sum(p_own, axis=-1, keepdims=True)
        acc = _dot(p_own.astype(BF16), vb[n * L:(n + 1) * L])
        if n > 0:
            p_past = jnp.exp(s_past - m)
            l = l + jnp.sum(p_past, axis=-1, keepdims=True)
            acc = acc + _dot(p_past.astype(BF16), vb[:n * L])
        o_ref[0, n * L:(n + 1) * L, :] = acc / l


def _moba_prompt(q, k_bf16, v_bf16, kmean, slopes_tile, n_heads):
    b, t, w = q.shape
    assert t % MOBA_BLOCK == 0 and w == n_heads * HEAD_DIM
    nblk = t // MOBA_BLOCK
    spec = pl.BlockSpec((1, t, HEAD_DIM), lambda i, h: (i, 0, h))
    kern = functools.partial(_moba_prompt_kernel, t_len=t)
    return pl.pallas_call(
        kern, out_shape=jax.ShapeDtypeStruct((b, t, w), F32), grid=(b, n_heads),
        in_specs=[pl.BlockSpec((1, 8, HEAD_DIM), lambda i, h: (h, 0, 0)), spec, spec, spec,
                  pl.BlockSpec((1, nblk, HEAD_DIM), lambda i, h: (i, 0, h))],
        out_specs=spec,
        compiler_params=_params("arbitrary", "arbitrary"), name="moba_prompt",
    )(slopes_tile, q, k_bf16, v_bf16, kmean)


def _mem_attn_kernel(q_ref, k_ref, v_ref, o_ref, *, n_heads, head_rows):
    scale = HEAD_DIM ** -0.5
    n_mem = k_ref.shape[1] // n_heads if head_rows else k_ref.shape[1]
    for h in range(n_heads):
        c = slice(h * HEAD_DIM, (h + 1) * HEAD_DIM)
        if head_rows:
            kh = k_ref[0, pl.ds(h, n_mem, stride=n_heads), :].astype(BF16)
            vh = v_ref[0, pl.ds(h, n_mem, stride=n_heads), :].astype(BF16)
        else:
            kh = k_ref[0, :, c].astype(BF16)
            vh = v_ref[0, :, c].astype(BF16)
        s = _dot_nt(q_ref[0, :, c].astype(BF16), kh) * scale
        m = jnp.max(s, axis=-1, keepdims=True)
        p = jnp.exp(s - m)
        l = jnp.sum(p, axis=-1, keepdims=True)
        o_ref[0, :, c] = _dot(p.astype(BF16), vh) / l


def _mem_attend(mq, mk, mv, n_heads, head_rows, tq_want=1024):
    b, t, w = mq.shape
    tq = _row_tile(t, tq_want)
    qspec = pl.BlockSpec((1, tq, w), lambda i, j: (i, j, 0))
    kspec = pl.BlockSpec((1,) + mk.shape[1:], lambda i, j: (i, 0, 0))
    return pl.pallas_call(
        functools.partial(_mem_attn_kernel, n_heads=n_heads, head_rows=head_rows),
        out_shape=jax.ShapeDtypeStruct((b, t, w), F32),
        grid=(b, t // tq), in_specs=[qspec, kspec, kspec], out_specs=qspec,
        compiler_params=_params("arbitrary", "arbitrary"), name="mem_attend",
    )(mq, mk, mv)


def _out_proj_kernel(x_ref, p_ref, a_ref, m_ref, w_ref, nw_ref, x1_ref, h_ref, mix_ref):
    o1 = p_ref.shape[1]
    o2 = o1 + a_ref.shape[1]
    mix_ref[:, 0:o1] = p_ref[...].astype(BF16)
    mix_ref[:, o1:o2] = a_ref[...].astype(BF16)
    mix_ref[:, o2:] = m_ref[...].astype(BF16)
    x1 = x_ref[...] + _dot(mix_ref[...], w_ref[...])
    x1_ref[...] = x1
    h_ref[...] = (x1 * _rms_scale(x1) * nw_ref[...]).astype(BF16)


def _out_proj(x2d, pool_y, attn_y, mem_y, w_out_bf16, norm_ffn, tm_want=512):
    n, d = x2d.shape
    tm = _row_tile(n, tm_want)
    widths = (pool_y.shape[1], attn_y.shape[1], mem_y.shape[1])
    assert sum(widths) == w_out_bf16.shape[0]
    row = lambda w: pl.BlockSpec((tm, w), lambda i: (i, 0))
    return pl.pallas_call(
        _out_proj_kernel,
        out_shape=[jax.ShapeDtypeStruct((n, d), F32), jax.ShapeDtypeStruct((n, d), BF16)],
        grid=(n // tm,),
        in_specs=[row(d), row(widths[0]), row(widths[1]), row(widths[2]), _whole_vmem(), _whole_vmem()],
        out_specs=[row(d), row(d)],
        scratch_shapes=[pltpu.VMEM((tm, sum(widths)), BF16)],
        compiler_params=_params("arbitrary"), name="out_proj",
    )(x2d, pool_y, attn_y, mem_y, w_out_bf16, norm_ffn.reshape(1, d))


def _ffn_up_kernel(h_ref, wg_ref, wv_ref, cwg_ref, cwv_ref, cbg_ref, cbv_ref, pg_ref, pv_ref,
                   a_ref, sg_ref, sv_ref, eg_ref, ev_ref, *, nseq, tt, fc, sub):
    j = pl.program_id(2)
    lo = CONV_PAD - CONV_CTX

    @pl.when(j == 0)
    def _():
        eg_ref[:, lo:CONV_PAD, :] = pg_ref[...]
        ev_ref[:, lo:CONV_PAD, :] = pv_ref[...]

    @pl.when(j > 0)
    def _():
        eg_ref[:, lo:CONV_PAD, :] = eg_ref[:, tt + lo:tt + CONV_PAD, :]
        ev_ref[:, lo:CONV_PAD, :] = ev_ref[:, tt + lo:tt + CONV_PAD, :]

    def conv(hm, r0, w_ref, cw_ref, cb_ref, ext_ref, state_ref):
        u = _dot(hm, w_ref[...]).reshape(nseq, sub, fc)
        ext_ref[:, CONV_PAD + r0:CONV_PAD + r0 + sub, :] = u
        if r0 + sub == tt:
            state_ref[...] = u[:, sub - CONV_CTX:, :]
        out = cb_ref[...].reshape(1, 1, fc)
        for kk in range(CONV_WIDTH):
            out = out + ext_ref[:, lo + r0 + kk:lo + r0 + kk + sub, :] * cw_ref[kk:kk + 1, :].reshape(1, 1, fc)
        return out

    for r0 in range(0, tt, sub):
        rows = slice(r0 * nseq, (r0 + sub) * nseq)
        hm = h_ref[rows, :]
        g = conv(hm, r0, wg_ref, cwg_ref, cbg_ref, eg_ref, sg_ref)
        val = conv(hm, r0, wv_ref, cwv_ref, cbv_ref, ev_ref, sv_ref)
        act = g * (1.0 / (1.0 + jnp.exp(-g))) * val
        a_ref[rows, :] = act.reshape(nseq * sub, fc).astype(BF16)


def _ffn_up_scan_kernel(pt_ref, *refs, n_scan, n_heads, **kw):
    del pt_ref
    n_ffn_in = 9
    page_refs = refs[n_ffn_in:n_ffn_in + n_scan]
    km_ref = refs[n_ffn_in + n_scan + 3]
    ppb = MOBA_BLOCK // PAGE_SIZE
    for n in range(n_scan // ppb):
        s = page_refs[n * ppb][...]
        for p in range(1, ppb):
            s = s + page_refs[n * ppb + p][...]
        km_ref[n * n_heads:(n + 1) * n_heads, :] = (
            jnp.sum(s.reshape(PAGE_SIZE, n_heads, HEAD_DIM), axis=0) * (1.0 / MOBA_BLOCK))
    _ffn_up_kernel(*refs[:n_ffn_in], *refs[n_ffn_in + n_scan:n_ffn_in + n_scan + 3], *refs[n_ffn_in + n_scan + 4:],
                   **kw)


def _ffn_up(h2d, nseq, tt, w_up_bf16, conv_w, conv_b, prev, fc=512, scan=None):
    n, d = h2d.shape
    f2 = w_up_bf16.shape[1]
    f = f2 // 2
    assert f % fc == 0
    nfc = f // fc
    n_seqs = prev.shape[0]
    seq_len = n // n_seqs
    assert seq_len % tt == 0 and (nseq == 1 or tt == seq_len) and n_seqs % nseq == 0
    tiles_per_seq = seq_len // tt
    rows = nseq * tt
    sub = _row_tile(tt, 256) if nseq == 1 else tt
    grid = (n_seqs // nseq, nfc, tiles_per_seq)
    kw = dict(nseq=nseq, tt=tt, fc=fc, sub=sub)
    wspec = lambda off: pl.BlockSpec((d, fc), lambda i, c, j, *_: (0, c + off))
    vspec = lambda r, off: pl.BlockSpec((r, fc), lambda i, c, j, *_: (0, c + off))
    pspec = lambda off: pl.BlockSpec((nseq, CONV_CTX, fc), lambda i, c, j, *_: (i, 0, c + off))
    sspec = pl.BlockSpec((nseq, CONV_CTX, fc), lambda i, c, j, *_: (i, 0, c))
    in_specs = [pl.BlockSpec((rows, d), lambda i, c, j, *_: (i * tiles_per_seq + j, 0)),
                wspec(0), wspec(nfc), vspec(CONV_WIDTH, 0), vspec(CONV_WIDTH, nfc),
                vspec(1, 0), vspec(1, nfc), pspec(0), pspec(nfc)]
    out_specs = [pl.BlockSpec((rows, fc), lambda i, c, j, *_: (i * tiles_per_seq + j, c)), sspec, sspec]
    out_shape = [jax.ShapeDtypeStruct((n, f), BF16),
                 jax.ShapeDtypeStruct((n_seqs, CONV_CTX, f), F32),
                 jax.ShapeDtypeStruct((n_seqs, CONV_CTX, f), F32)]
    scratch = [pltpu.VMEM((nseq, CONV_PAD + tt, fc), F32), pltpu.VMEM((nseq, CONV_PAD + tt, fc), F32)]
    args = (h2d, w_up_bf16, w_up_bf16, conv_w, conv_w, conv_b.reshape(1, f2), conv_b.reshape(1, f2), prev, prev)
    if scan is None:
        return pl.pallas_call(
            functools.partial(_ffn_up_kernel, **kw), out_shape=out_shape, grid=grid, in_specs=in_specs,
            out_specs=out_specs, scratch_shapes=scratch,
            compiler_params=_params("arbitrary", "arbitrary", "arbitrary"), name="ffn_up")(*args)

    cache_flat, pages, n_heads = scan
    ppb = MOBA_BLOCK // PAGE_SIZE
    n_steps = grid[0] * grid[1] * grid[2]
    total = pages.shape[0]
    assert total % ppb == 0
    n_scan = -(-total // (n_steps * ppb)) * ppb
    page_rows = PAGE_SIZE * n_heads
    km_rows = n_scan // ppb * n_heads

    def step_of(i, c, j):
        return (i * nfc + c) * tiles_per_seq + j

    def page_spec(p):
        return pl.BlockSpec((page_rows, HEAD_DIM),
                            lambda i, c, j, pt: (pt[jnp.minimum(step_of(i, c, j) * n_scan + p, total - 1)], 0))

    grid_spec = pltpu.PrefetchScalarGridSpec(
        num_scalar_prefetch=1, grid=grid,
        in_specs=in_specs + [page_spec(p) for p in range(n_scan)],
        out_specs=out_specs + [pl.BlockSpec((km_rows, HEAD_DIM), lambda i, c, j, pt: (step_of(i, c, j), 0))],
        scratch_shapes=scratch)
    act, sg, sv, km = pl.pallas_call(
        functools.partial(_ffn_up_scan_kernel, n_scan=n_scan, n_heads=n_heads, **kw),
        out_shape=out_shape + [jax.ShapeDtypeStruct((n_steps * km_rows, HEAD_DIM), F32)], grid_spec=grid_spec,
        compiler_params=_params("arbitrary", "arbitrary", "arbitrary"), name="ffn_up_scan",
    )(pages, *args, *([cache_flat] * n_scan))
    return act, sg, sv, km[:total // ppb * n_heads]


def _ffn_down_kernel(a_ref, x_ref, w_ref, y_ref):
    y_ref[...] = x_ref[...] + _dot(a_ref[...], w_ref[...])


def _ffn_down(act, x1, w_down_bf16, tm_want=512):
    n, f = act.shape
    d = x1.shape[1]
    tm = _row_tile(n, tm_want)
    return pl.pallas_call(
        _ffn_down_kernel, out_shape=jax.ShapeDtypeStruct((n, d), F32), grid=(n // tm,),
        in_specs=[pl.BlockSpec((tm, f), lambda i: (i, 0)), pl.BlockSpec((tm, d), lambda i: (i, 0)), _whole_vmem()],
        out_specs=pl.BlockSpec((tm, d), lambda i: (i, 0)),
        compiler_params=_params("arbitrary"), name="ffn_down",
    )(act, x1, w_down_bf16)


TOPK_LANES = 4


def _gate_topk_kernel(q_ref, km_ref, idx_ref, *, n_heads, n_sel):
    q = q_ref[0]
    tn = q.shape[0]
    n_full = km_ref.shape[1] // n_heads
    lane = lax.broadcasted_iota(jnp.int32, (tn, n_full), 1).astype(F32)
    out_lane = lax.broadcasted_iota(jnp.int32, (tn, HEAD_DIM), 1)
    out = jnp.zeros((tn, HEAD_DIM), F32)
    for h in range(n_heads):
        c = slice(h * HEAD_DIM, (h + 1) * HEAD_DIM)
        km = km_ref[0, pl.ds(h, n_full, stride=n_heads), :]
        g = _dot_nt(q[:, c], km, precision=lax.Precision.HIGHEST)
        for r in range(n_sel):
            m = jnp.max(g, axis=-1, keepdims=True)
            idx = jnp.min(jnp.where(g == m, lane, float(n_full)), axis=-1, keepdims=True)
            out = jnp.where(out_lane == h * TOPK_LANES + r, idx, out)
            g = jnp.where(lane == idx, -jnp.inf, g)
    idx_ref[0] = out.astype(jnp.int32)


def _gate_topk(q_s, kmean, n_heads, n_sel):
    db, tn, w = q_s.shape
    rows = kmean.shape[1]
    assert n_heads * TOPK_LANES <= HEAD_DIM and n_sel <= TOPK_LANES
    kern = functools.partial(_gate_topk_kernel, n_heads=n_heads, n_sel=n_sel)
    return pl.pallas_call(
        kern, out_shape=jax.ShapeDtypeStruct((db, tn, HEAD_DIM), jnp.int32), grid=(db,),
        in_specs=[pl.BlockSpec((1, tn, w), lambda b: (b, 0, 0)),
                  pl.BlockSpec((1, rows, HEAD_DIM), lambda b: (b, 0, 0))],
        out_specs=pl.BlockSpec((1, tn, HEAD_DIM), lambda b: (b, 0, 0)),
        compiler_params=_params("arbitrary"), name="gate_topk",
    )(q_s, kmean)


def _moba_sample_kernel(pt_ref, idx_ref, slope_ref, q_ref, kn_ref, vn_ref, ck_ref, cv_ref, o_ref,
                        kbuf, vbuf, sem, *, tn, n_sel, n_heads, past):
    L = MOBA_BLOCK
    ppb = L // PAGE_SIZE
    n_slots = tn * n_sel * ppb
    b = pl.program_id(0)
    h = pl.program_id(1)
    step = b * n_heads + h
    n_steps = pl.num_programs(0) * n_heads
    cur = lax.rem(step, 2)
    scale = HEAD_DIM ** -0.5

    def block_of(bb, hh, t, r):
        return idx_ref[((bb * n_heads + hh) * tn + t) * n_sel + r]

    def gathers(bb, hh, buf_slot):
        out = []
        for t in range(tn):
            for r in range(n_sel):
                blk = block_of(bb, hh, t, r)
                for p in range(ppb):
                    page = pt_ref[bb, blk * ppb + p]
                    i = (t * n_sel + r) * ppb + p
                    out.append(pltpu.make_async_copy(ck_ref.at[page, :, hh, :], kbuf.at[buf_slot, i],
                                                     sem.at[0, buf_slot]))
                    out.append(pltpu.make_async_copy(cv_ref.at[page, :, hh, :], vbuf.at[buf_slot, i],
                                                     sem.at[1, buf_slot]))
        return out

    @pl.when(step == 0)
    def _():
        for c in gathers(b, h, cur):
            c.start()

    @pl.when(step + 1 < n_steps)
    def _():
        nxt = step + 1
        for c in gathers(nxt // n_heads, lax.rem(nxt, n_heads), 1 - cur):
            c.start()

    for c in gathers(b, h, cur):
        c.wait()

    slope = slope_ref[0, 0:1, 0:1]
    qb = q_ref[0].astype(BF16)
    n_keys = n_slots * PAGE_SIZE
    s = _dot_nt(qb, kbuf[cur].reshape(n_keys, HEAD_DIM).astype(BF16)) * scale
    lane = lax.broadcasted_iota(jnp.int32, (tn, PAGE_SIZE), 1)
    k_pos = jnp.concatenate(
        [block_of(b, h, t, r) * L + p * PAGE_SIZE + lane for t in range(tn) for r in range(n_sel) for p in range(ppb)],
        axis=1)
    row = lax.broadcasted_iota(jnp.int32, (tn, n_keys), 0)
    col = lax.broadcasted_iota(jnp.int32, (tn, n_keys), 1)
    per_query = n_sel * L
    mine = (col >= row * per_query) & (col < row * per_query + per_query)
    s = jnp.where(mine, s - slope * (past + row - k_pos).astype(F32), NEG_INF)
    row_n = lax.broadcasted_iota(jnp.int32, (tn, tn), 0)
    lane_n = lax.broadcasted_iota(jnp.int32, (tn, tn), 1)
    s_own = _dot_nt(qb, kn_ref[0]) * scale - slope * (row_n - lane_n).astype(F32)
    s_own = jnp.where(lane_n <= row_n, s_own, NEG_INF)
    m = jnp.maximum(jnp.max(s, axis=-1, keepdims=True), jnp.max(s_own, axis=-1, keepdims=True))
    p = jnp.exp(s - m)
    p_own = jnp.exp(s_own - m)
    l = jnp.sum(p, axis=-1, keepdims=True) + jnp.sum(p_own, axis=-1, keepdims=True)
    acc = _dot(p.astype(BF16), vbuf[cur].reshape(n_keys, HEAD_DIM).astype(BF16)) + _dot(p_own.astype(BF16), vn_ref[0])
    o_ref[0] = acc / l


def _moba_sample(q_s, k_new_bf16, v_new_bf16, cache_k4, cache_v4, page_table, idx, slopes_tile, n_heads):
    db, tn, w = q_s.shape
    n_sel = idx.shape[-1]
    ppb = MOBA_BLOCK // PAGE_SIZE
    past = page_table.shape[1] * PAGE_SIZE
    n_slots = tn * n_sel * ppb
    new_spec = pl.BlockSpec((1, tn, HEAD_DIM), lambda b, h, pt, ix: (b, 0, h))
    hbm = pl.BlockSpec(memory_space=pl.ANY)
    grid_spec = pltpu.PrefetchScalarGridSpec(
        num_scalar_prefetch=2, grid=(db, n_heads),
        in_specs=[pl.BlockSpec((1, 8, HEAD_DIM), lambda b, h, pt, ix: (h, 0, 0)), new_spec, new_spec, new_spec,
                  hbm, hbm],
        out_specs=new_spec,
        scratch_shapes=[pltpu.VMEM((2, n_slots, PAGE_SIZE, HEAD_DIM), F32),
                        pltpu.VMEM((2, n_slots, PAGE_SIZE, HEAD_DIM), F32),
                        pltpu.SemaphoreType.DMA((2, 2))])
    kern = functools.partial(_moba_sample_kernel, tn=tn, n_sel=n_sel, n_heads=n_heads, past=past)
    return pl.pallas_call(
        kern, out_shape=jax.ShapeDtypeStruct((db, tn, w), F32), grid_spec=grid_spec,
        compiler_params=_params("arbitrary", "arbitrary"), name="moba_sample",
    )(page_table, idx.reshape(-1), slopes_tile, q_s, k_new_bf16, v_new_bf16, cache_k4, cache_v4)


def _last_rows(prev, new, n):
    t = new.shape[1]
    if t >= n:
        return new[:, t - n:]
    return jnp.concatenate([prev[:, prev.shape[1] - (n - t):], new], axis=1)


def kernel(x_prompt, x_sample, cache_k, cache_v, cache_mem_k, cache_mem_v, state_pool, state_conv, page_table, mem_prompt, norm_mix, w_in, pool_w, pool_scale, q_gain, k_gain, norm_mem, w_mem_kv, mq_gain, mk_gain, w_out, norm_ffn, w_up, conv_w, conv_b, w_down):
    depth = w_in.shape[0]
    assert depth == 1
    bsz, seq, d = x_prompt.shape
    db, tn, _ = x_sample.shape
    pool_width = pool_w.shape[1] * pool_w.shape[2]
    n_heads = cache_k.shape[3]
    attn_width = n_heads * HEAD_DIM
    mem_width = MEM_HEADS * HEAD_DIM
    n_mem = mem_prompt.shape[1]
    n_pool = cache_k.shape[1]
    n_pages = page_table.shape[1]
    past = n_pages * PAGE_SIZE
    n_full = past // MOBA_BLOCK
    assert past % MOBA_BLOCK == 0
    f2 = w_up.shape[2]

    slopes = jnp.exp2(-8.0 * jnp.arange(1, n_heads + 1, dtype=F32) / n_heads)
    slopes_tile = jnp.broadcast_to(slopes[:, None, None], (n_heads, 8, HEAD_DIM))

    l = 0
    w_in_b = w_in[l].astype(BF16)
    w_out_b = w_out[l].astype(BF16)
    w_up_b = w_up[l].astype(BF16)
    w_down_b = w_down[l].astype(BF16)
    w_mem_b = w_mem_kv[l].astype(BF16)
    pool_w_b = pool_w[l].astype(BF16)

    o1 = pool_width
    o2 = o1 + attn_width
    o3 = o2 + attn_width
    o4 = o3 + attn_width
    def in_segs(k_mode):
        return [(0, pool_width, None, FLAT), (o1, attn_width, 0, FLAT), (o2, attn_width, 1, k_mode),
                (o3, attn_width, None, HEADS), (o4, mem_width, 2, FLAT)]

    in_gains = [q_gain[l], k_gain[l], mq_gain[l]]

    def mix_and_ffn(x2d, pool_y, attn_y, mem_y, conv_prev, nseq, tt, scan=None):
        x1, h = _out_proj(x2d, pool_y, attn_y, mem_y, w_out_b, norm_ffn[l])
        act, sg, sv, *extra = _ffn_up(h, nseq, tt, w_up_b, conv_w[l], conv_b[l], conv_prev, scan=scan)
        y = _ffn_down(act, x1, w_down_b)
        return (y, jnp.concatenate([sg, sv], axis=-1), *extra)

    xp2 = x_prompt.reshape(bsz * seq, d)
    u, q, k, k_b, k_mean, v, v_b, mq = _norm_proj(xp2, norm_mix[l], w_in_b, in_gains, in_segs(HEADS_MEAN))
    u3 = u.reshape(bsz, seq, pool_width)
    pool_y = _pool_mix(u3, jnp.zeros((bsz, POOL_PAD, pool_width), F32), 0, pool_w_b, pool_scale[l])
    attn_y = _moba_prompt(q.reshape(bsz, seq, attn_width), k_b.reshape(bsz, seq, attn_width),
                          v_b.reshape(bsz, seq, attn_width), k_mean.reshape(bsz, seq // MOBA_BLOCK, attn_width),
                          slopes_tile, n_heads)
    mk, mk_b, mv, mv_b = _norm_proj(mem_prompt.reshape(bsz * n_mem, d), norm_mem[l], w_mem_b, [mk_gain[l]],
                                    [(0, mem_width, 0, HEADS), (mem_width, mem_width, None, HEADS)])
    mem_y = _mem_attend(mq.reshape(bsz, seq, mem_width), mk_b.reshape(bsz, n_mem, mem_width),
                        mv_b.reshape(bsz, n_mem, mem_width), MEM_HEADS, head_rows=False, tq_want=2048)
    cache_k_flat = cache_k[l].reshape(n_pool * PAGE_SIZE * n_heads, HEAD_DIM)
    y_p, conv_st_p, kmean = mix_and_ffn(
        xp2, pool_y.reshape(bsz * seq, pool_width), attn_y.reshape(bsz * seq, attn_width),
        mem_y.reshape(bsz * seq, mem_width), jnp.zeros((bsz, CONV_CTX, f2), F32), 1, _row_tile(seq, 1024),
        scan=(cache_k_flat, page_table.reshape(-1), n_heads))
    pool_st_p = _last_rows(jnp.zeros((bsz, POOL_CTX, pool_width), F32), u3, POOL_CTX)

    xs2 = x_sample.reshape(db * tn, d)
    us, qs, ks, ks_b, vs, vs_b, mqs = _norm_proj(xs2, norm_mix[l], w_in_b, in_gains, in_segs(HEADS))
    us3 = us.reshape(db, tn, pool_width)
    prev_pool = jnp.concatenate([jnp.zeros((db, POOL_PAD - POOL_CTX, pool_width), F32), state_pool[l]], axis=1)
    pool_ys = _pool_mix(us3, prev_pool, past, pool_w_b, pool_scale[l])
    qs3, ks3, vs3 = (a.reshape(db, tn, attn_width) for a in (qs, ks_b, vs_b))
    n_sel = min(MOBA_TOPK, n_full)
    assert n_sel > 0
    idx_lanes = _gate_topk(qs3, kmean.reshape(db, n_full * n_heads, HEAD_DIM), n_heads, n_sel)
    idx = idx_lanes[:, :, :n_heads * TOPK_LANES].reshape(db, tn, n_heads, TOPK_LANES)[..., :n_sel]
    attn_ys = _moba_sample(qs3, ks3, vs3, cache_k[l], cache_v[l], page_table, idx.transpose(0, 2, 1, 3),
                           slopes_tile, n_heads)
    mem_ys = _mem_attend(mqs.reshape(db, tn, mem_width), cache_mem_k[l].reshape(db, n_mem * MEM_HEADS, HEAD_DIM),
                         cache_mem_v[l].reshape(db, n_mem * MEM_HEADS, HEAD_DIM), MEM_HEADS, head_rows=True)
    y_s, conv_st_s = mix_and_ffn(xs2, pool_ys.reshape(db * tn, pool_width), attn_ys.reshape(db * tn, attn_width),
                                 mem_ys.reshape(db * tn, mem_width), state_conv[l], db, tn)
    pool_st_s = _last_rows(state_pool[l], us3, POOL_CTX)

    hd = HEAD_DIM
    return (y_p.reshape(bsz, seq, d), y_s.reshape(db, tn, d),
            k.reshape(1, bsz, seq, n_heads, hd), v.reshape(1, bsz, seq, n_heads, hd),
            mk.reshape(1, bsz, n_mem, MEM_HEADS, hd), mv.reshape(1, bsz, n_mem, MEM_HEADS, hd),
            pool_st_p[None], conv_st_p[None],
            ks.reshape(1, db, tn, n_heads, hd), vs.reshape(1, db, tn, n_heads, hd),
            pool_st_s[None], conv_st_s[None])
```
